```python
import math
import jax, jax.numpy as jnp
from jax import lax
import numpy as np

D_MODEL = 1024
BATCH = 16
SEQ = 2048
DEPTH = 4
DEC_BATCH = 8
DEC_SEQ = 2048
PAST_LEN = 128

RET_HEADS = 4
RET_QK_DIM = 128
RET_V_DIM = 256
RET_QK_W = RET_HEADS * RET_QK_DIM
RET_V_W = RET_HEADS * RET_V_DIM
RET_CHUNK = 128
MLA_HEADS = 8
MLA_NOPE = 128
MLA_ROPE = 64
MLA_V = 128
Q_LORA = 256
KV_LORA = 128
MLA_QK = MLA_NOPE + MLA_ROPE
MLA_V_W = MLA_HEADS * MLA_V
Q_BLOCK = 128
IN_SPLITS = (RET_QK_W, RET_QK_W, RET_V_W, RET_V_W, Q_LORA, KV_LORA, MLA_ROPE, D_MODEL, D_MODEL)
IN_WIDTH = sum(IN_SPLITS)
D_FF = 2816
CONV_W = 3
ROPE_THETA = 10000.0
LN_EPS = 1e-5
RMS_EPS = 1e-6
DEEPNORM_ALPHA = (2.0 * DEPTH) ** 0.25
DEEPNORM_BETA = (8.0 * DEPTH) ** -0.25

kernel_name = "hybrid_retention_mla_convffn_encoder"


def _layernorm(x):
    xf = x.astype(jnp.float32)
    mu = jnp.mean(xf, axis=-1, keepdims=True)
    xc = xf - mu
    var = jnp.mean(xc * xc, axis=-1, keepdims=True)
    return (xc * lax.rsqrt(var + LN_EPS)).astype(x.dtype)


def _rmsnorm(x, g):
    xf = x.astype(jnp.float32)
    y = xf * lax.rsqrt(jnp.mean(xf * xf, axis=-1, keepdims=True) + RMS_EPS)
    return y.astype(x.dtype) * g


def _rope_tables(seq, dim):
    inv = 1.0 / (ROPE_THETA ** (jnp.arange(0, dim, 2, dtype=jnp.float32) / dim))
    ang = jnp.arange(seq, dtype=jnp.float32)[:, None] * inv[None, :]
    return jnp.cos(ang), jnp.sin(ang)


def _apply_rope(x, cos, sin):
    extra = x.ndim - 3
    c = cos.reshape(cos.shape[0], *([1] * extra), cos.shape[-1])
    s = sin.reshape(sin.shape[0], *([1] * extra), sin.shape[-1])
    xf = x.astype(jnp.float32)
    x1, x2 = jnp.split(xf, 2, axis=-1)
    return jnp.concatenate([x1 * c - x2 * s, x2 * c + x1 * s], axis=-1).astype(x.dtype)


def _retention_one_direction(q, k, v, log_gamma, strict):
    B, S, H, DK = q.shape
    DV = v.shape[-1]
    C = RET_CHUNK
    N = S // C
    qc = q.reshape(B, N, C, H, DK)
    kc = k.reshape(B, N, C, H, DK)
    vc = v.reshape(B, N, C, H, DV)
    idx = jnp.arange(C, dtype=jnp.float32)
    diff = idx[:, None] - idx[None, :]
    keep = (diff > 0) if strict else (diff >= 0)
    decay = jnp.where(keep[None], jnp.exp(log_gamma[:, None, None] * jnp.maximum(diff, 0.0)[None]), 0.0)
    scores = jnp.einsum('bnqhd,bnkhd->bnhqk', qc, kc) * decay[None, None]
    intra = jnp.einsum('bnhqk,bnkhe->bnqhe', scores, vc)
    q_decay = jnp.exp(log_gamma[None, :] * (idx[:, None] + 1.0))
    k_decay = jnp.exp(log_gamma[None, :] * (C - 1.0 - idx)[:, None])
    chunk_decay = jnp.exp(log_gamma * C)
    kv_chunk = jnp.einsum('bnchd,bnche->nbhde', kc * k_decay[None, None, :, :, None], vc)

    def step(state, kv_n):
        return chunk_decay[None, :, None, None] * state + kv_n, state

    _, states = lax.scan(step, jnp.zeros((B, H, DK, DV), jnp.float32), kv_chunk)
    cross = jnp.einsum('bnchd,nbhde->bnche', qc * q_decay[None, None, :, :, None], states)
    return (intra + cross).reshape(B, S, H, DV)


def _bidirectional_retention(q, k, v, decay_fwd, decay_bwd):
    dt = v.dtype
    qf, kf, vf = q.astype(jnp.float32), k.astype(jnp.float32), v.astype(jnp.float32)
    lg_f = jnp.log(jax.nn.sigmoid(decay_fwd.astype(jnp.float32)))
    lg_b = jnp.log(jax.nn.sigmoid(decay_bwd.astype(jnp.float32)))
    o_f = _retention_one_direction(qf, kf, vf, lg_f, False)
    o_b = jnp.flip(_retention_one_direction(jnp.flip(qf, 1), jnp.flip(kf, 1), jnp.flip(vf, 1), lg_b, True), 1)
    return (o_f + o_b).astype(dt)


def _mla_attention(q_nope, q_rope, k_nope, k_rope, v):
    B, S, H, _ = q_nope.shape
    NB = S // Q_BLOCK
    scale = MLA_QK ** -0.5

    def blockify(t):
        return jnp.moveaxis(t.reshape(B, NB, Q_BLOCK, *t.shape[2:]), 1, 0)

    def one_block(args):
        qn, qr = args
        s = jnp.einsum('bqhd,bkhd->bhqk', qn, k_nope) + jnp.einsum('bqhr,bkr->bhqk', qr, k_rope)
        p = jax.nn.softmax(s.astype(jnp.float32) * scale, axis=-1).astype(v.dtype)
        return jnp.einsum('bhqk,bkhe->bqhe', p, v)

    out = lax.map(one_block, (blockify(q_nope), blockify(q_rope)))
    return jnp.moveaxis(out, 0, 1).reshape(B, S, H * MLA_V)


def _dwconv3(a, w, b):
    ap = jnp.pad(a, ((0, 0), (1, 1), (0, 0)))
    return ap[:, :-2] * w[0] + ap[:, 1:-1] * w[1] + ap[:, 2:] * w[2] + b


def _encoder_layer(x, c, w_ada, b_ada, w_in, ret_decay_fwd, ret_decay_bwd, ret_gn_g, w_ret_o,
                   q_norm_g, kv_norm_g, w_uq, w_uk, w_uv, w_mla_o, w_out, ln1_g, ln1_b,
                   w_up, conv_w, conv_b, w_down, ln2_g, ln2_b):
    B, S, D = x.shape
    ada = jax.nn.silu(c) @ w_ada + b_ada
    sh1, sc1, g1, sh2, sc2, g2 = jnp.split(ada[:, None, :], 6, axis=-1)
    cos_r, sin_r = _rope_tables(S, RET_QK_DIM)
    cos_m, sin_m = _rope_tables(S, MLA_ROPE)

    h = _layernorm(x) * (1.0 + sc1) + sh1
    proj = h @ w_in
    offs, acc = [], 0
    for w in IN_SPLITS[:-1]:
        acc += w
        offs.append(acc)
    rq, rk, rv, rg, dq, dkv, kr, gA, gB = jnp.split(proj, offs, axis=-1)

    rq = _apply_rope(rq.reshape(B, S, RET_HEADS, RET_QK_DIM), cos_r, sin_r)
    rk = _apply_rope(rk.reshape(B, S, RET_HEADS, RET_QK_DIM), cos_r, sin_r) * (RET_QK_DIM ** -0.5)
    rv = rv.reshape(B, S, RET_HEADS, RET_V_DIM)
    ro = _bidirectional_retention(rq, rk, rv, ret_decay_fwd, ret_decay_bwd)
    ro = _layernorm(ro).reshape(B, S, RET_V_W) * ret_gn_g
    y_a = (jax.nn.silu(rg) * ro) @ w_ret_o

    cq = _rmsnorm(dq, q_norm_g)
    qm = (cq @ w_uq).reshape(B, S, MLA_HEADS, MLA_QK)
    q_nope, q_rope = qm[..., :MLA_NOPE], _apply_rope(qm[..., MLA_NOPE:], cos_m, sin_m)
    ckv = _rmsnorm(dkv, kv_norm_g)
    k_nope = (ckv @ w_uk).reshape(B, S, MLA_HEADS, MLA_NOPE)
    v_m = (ckv @ w_uv).reshape(B, S, MLA_HEADS, MLA_V)
    k_rope = _apply_rope(kr, cos_m, sin_m)
    y_b = _mla_attention(q_nope, q_rope, k_nope, k_rope, v_m) @ w_mla_o

    merged = jax.nn.sigmoid(gA) * y_a + jax.nn.sigmoid(gB) * y_b
    f = merged @ w_out
    x = _layernorm(DEEPNORM_ALPHA * x + (1.0 + g1) * f) * ln1_g + ln1_b

    h = _layernorm(x) * (1.0 + sc2) + sh2
    a, bgate = jnp.split(h @ w_up, 2, axis=-1)
    a = _dwconv3(a, conv_w, conv_b)
    y = (jax.nn.gelu(a, approximate=False) * bgate) @ w_down
    x = _layernorm(DEEPNORM_ALPHA * x + (1.0 + g2) * y) * ln2_g + ln2_b
    return x


def setup_inputs(seed: int = 0) -> dict:
    key = jax.random.key(seed)
    ks = jax.random.split(key, 32)
    nrm = lambda k, shp, s: jax.random.normal(k, shp, jnp.float32) * s
    L, D = DEPTH, D_MODEL
    base_decay = jnp.log(2.0 ** (5.0 + jnp.arange(RET_HEADS, dtype=jnp.float32)) - 1.0)
    return {
        "x_prompt": nrm(ks[0], (BATCH, SEQ, D), 1.0),
        "x_sample": nrm(ks[1], (DEC_BATCH, DEC_SEQ, D), 1.0),
        "c_prompt": nrm(ks[2], (BATCH, D), 1.0),
        "c_sample": nrm(ks[3], (DEC_BATCH, D), 1.0),
        "w_ada": nrm(ks[4], (L, D, 6 * D), 0.1 * D ** -0.5),
        "b_ada": nrm(ks[5], (L, 6 * D), 0.01),
        "w_in": nrm(ks[6], (L, D, IN_WIDTH), D ** -0.5),
        "ret_decay_fwd": base_decay[None] + nrm(ks[7], (L, RET_HEADS), 0.01),
        "ret_decay_bwd": base_decay[None] + nrm(ks[8], (L, RET_HEADS), 0.01),
        "ret_gn_g": 1.0 + nrm(ks[9], (L, RET_V_W), 0.01),
        "w_ret_o": nrm(ks[10], (L, RET_V_W, D), RET_V_W ** -0.5),
        "q_norm_g": 1.0 + nrm(ks[11], (L, Q_LORA), 0.01),
        "kv_norm_g": 1.0 + nrm(ks[12], (L, KV_LORA), 0.01),
        "w_uq": nrm(ks[13], (L, Q_LORA, MLA_HEADS * MLA_QK), Q_LORA ** -0.5),
        "w_uk": nrm(ks[14], (L, KV_LORA, MLA_HEADS * MLA_NOPE), KV_LORA ** -0.5),
        "w_uv": nrm(ks[15], (L, KV_LORA, MLA_V_W), KV_LORA ** -0.5),
        "w_mla_o": nrm(ks[16], (L, MLA_V_W, D), MLA_V_W ** -0.5),
        "w_out": nrm(ks[17], (L, D, D), DEEPNORM_BETA * D ** -0.5),
        "ln1_g": 1.0 + nrm(ks[18], (L, D), 0.01),
        "ln1_b": nrm(ks[19], (L, D), 0.01),
        "w_up": nrm(ks[20], (L, D, 2 * D_FF), D ** -0.5),
        "conv_w": nrm(ks[21], (L, CONV_W, D_FF), CONV_W ** -0.5),
        "conv_b": nrm(ks[22], (L, D_FF), 0.01),
        "w_down": nrm(ks[23], (L, D_FF, D), DEEPNORM_BETA * D_FF ** -0.5),
        "ln2_g": 1.0 + nrm(ks[24], (L, D), 0.01),
        "ln2_b": nrm(ks[25], (L, D), 0.01),
    }


def reference(x_prompt, x_sample, c_prompt, c_sample, w_ada, b_ada, w_in, ret_decay_fwd, ret_decay_bwd,
              ret_gn_g, w_ret_o, q_norm_g, kv_norm_g, w_uq, w_uk, w_uv, w_mla_o, w_out, ln1_g, ln1_b,
              w_up, conv_w, conv_b, w_down, ln2_g, ln2_b):
    params = (w_ada, b_ada, w_in, ret_decay_fwd, ret_decay_bwd, ret_gn_g, w_ret_o, q_norm_g, kv_norm_g,
              w_uq, w_uk, w_uv, w_mla_o, w_out, ln1_g, ln1_b, w_up, conv_w, conv_b, w_down, ln2_g, ln2_b)
    y_prompt, y_sample = x_prompt, x_sample
    for l in range(DEPTH):
        layer_params = [p[l] for p in params]
        y_prompt = _encoder_layer(y_prompt, c_prompt, *layer_params)
        y_sample = _encoder_layer(y_sample, c_sample, *layer_params)
    return (y_prompt, y_sample)
```

```python
import functools
import math

import jax
import jax.numpy as jnp
from jax import lax
from jax.experimental import pallas as pl
from jax.experimental.pallas import tpu as pltpu

D_MODEL = 1024
SEQ = 2048
DEPTH = 4
RET_HEADS = 4
RET_QK_DIM = 128
RET_V_DIM = 256
RET_QK_W = RET_HEADS * RET_QK_DIM
RET_V_W = RET_HEADS * RET_V_DIM
RET_CHUNK = 128
MLA_HEADS = 8
MLA_NOPE = 128
MLA_ROPE = 64
MLA_V = 128
Q_LORA = 256
KV_LORA = 128
MLA_QK = MLA_NOPE + MLA_ROPE
MLA_V_W = MLA_HEADS * MLA_V
MLA_HEAD_PAD = 256
D_FF = 2816
ROPE_THETA = 10000.0
LN_EPS = 1e-5
RMS_EPS = 1e-6
DEEPNORM_ALPHA = (2.0 * DEPTH) ** 0.25

IN_EXT_W = 5632
OFF_RQ, OFF_RK, OFF_RV, OFF_RG, OFF_GA, OFF_GB = 0, 512, 1024, 2048, 3072, 4096
OFF_DQ, OFF_DKV, OFF_KR = 5120, 5376, 5504

V7X_VMEM_LIMIT_BYTES = 56 * 1024 * 1024
ROW_TILE = 512
MM_COL_TILE = 512
ATT_Q_TILE = 512

BF16 = jnp.bfloat16
F32 = jnp.float32


def _cparams(*sem):
    return pltpu.CompilerParams(dimension_semantics=sem, vmem_limit_bytes=V7X_VMEM_LIMIT_BYTES)


def _layernorm_rows(x):
    mu = jnp.mean(x, axis=-1, keepdims=True)
    xc = x - mu
    var = jnp.mean(xc * xc, axis=-1, keepdims=True)
    return xc * lax.rsqrt(var + LN_EPS)


def _ada_kernel(c_ref, w_ref, b_ref, o_ref):
    c = c_ref[...]
    s = (c * jax.nn.sigmoid(c)).astype(BF16)
    o_ref[...] = jnp.dot(s, w_ref[...].astype(BF16), preferred_element_type=F32) + b_ref[...]


def _ada_table(c, w_ada, b_ada):
    nb = c.shape[0]
    return pl.pallas_call(
        _ada_kernel,
        grid=(DEPTH, 6),
        in_specs=[
            pl.BlockSpec((nb, D_MODEL), lambda l, j: (0, 0)),
            pl.BlockSpec((None, D_MODEL, D_MODEL), lambda l, j: (l, 0, j)),
            pl.BlockSpec((None, 1, D_MODEL), lambda l, j: (l, 0, j)),
        ],
        out_specs=pl.BlockSpec((None, None, nb, D_MODEL), lambda l, j: (l, j, 0, 0)),
        out_shape=jax.ShapeDtypeStruct((DEPTH, 6, nb, D_MODEL), F32),
        compiler_params=_cparams("parallel", "parallel"),
        name="ada_table",
    )(c, w_ada, b_ada.reshape(DEPTH, 1, 6 * D_MODEL))


def _ln_mod_mm_kernel(x_ref, sc_ref, sh_ref, w_ref, o_ref):
    h = _layernorm_rows(x_ref[...]) * (1.0 + sc_ref[...]) + sh_ref[...]
    hb = h.astype(BF16)
    n_out = o_ref.shape[1]
    for n in range(n_out // MM_COL_TILE):
        cols = slice(n * MM_COL_TILE, (n + 1) * MM_COL_TILE)
        o_ref[:, cols] = jnp.dot(hb, w_ref[:, cols], preferred_element_type=F32).astype(o_ref.dtype)


def _ln_mod_mm(x, sc, sh, w, name):
    t = x.shape[0]
    n_out = w.shape[1]
    tiles_per_seq = SEQ // ROW_TILE
    mod_spec = pl.BlockSpec((None, 1, D_MODEL), lambda i: (i // tiles_per_seq, 0, 0))
    return pl.pallas_call(
        _ln_mod_mm_kernel,
        grid=(t // ROW_TILE,),
        in_specs=[
            pl.BlockSpec((ROW_TILE, D_MODEL), lambda i: (i, 0)),
            mod_spec, mod_spec,
            pl.BlockSpec((D_MODEL, n_out), lambda i: (0, 0)),
        ],
        out_specs=pl.BlockSpec((ROW_TILE, n_out), lambda i: (i, 0)),
        out_shape=jax.ShapeDtypeStruct((t, n_out), BF16),
        compiler_params=_cparams("parallel"),
        name=name,
    )(x, sc, sh, w)


def _rms_rows(x, g):
    return x * lax.rsqrt(jnp.mean(x * x, axis=-1, keepdims=True) + RMS_EPS) * g


def _mla_proj_kernel(dq_ref, dkv_ref, kr_ref, cs_ref, qg_ref, kvg_ref, wq_ref, wkv_ref,
                     qf_ref, kf_ref, v_ref):
    scale = MLA_QK ** -0.5
    cs = cs_ref[...]
    cq = _rms_rows(dq_ref[...].astype(F32), qg_ref[...]).astype(BF16)
    ckv = _rms_rows(dkv_ref[...].astype(F32), kvg_ref[...]).astype(BF16)
    tk = kr_ref[...].astype(F32) * cs
    lane = lax.broadcasted_iota(jnp.int32, tk.shape, 1)
    k_rope = jnp.where(lane < MLA_ROPE, tk + pltpu.roll(tk, MLA_ROPE, axis=1), 0.0).astype(BF16)
    kv = jnp.dot(ckv, wkv_ref[...], preferred_element_type=F32)
    v_ref[...] = kv[:, MLA_V_W:].astype(BF16)
    for h in range(MLA_HEADS):
        base = h * MLA_HEAD_PAD
        qh = jnp.dot(cq, wq_ref[:, base:base + MLA_HEAD_PAD], preferred_element_type=F32)
        tq = qh[:, MLA_NOPE:] * cs
        q_rope = tq + pltpu.roll(tq, MLA_ROPE, axis=1)
        qf_ref[:, base:base + MLA_NOPE] = (qh[:, :MLA_NOPE] * scale).astype(BF16)
        qf_ref[:, base + MLA_NOPE:base + MLA_HEAD_PAD] = (q_rope * scale).astype(BF16)
        kf_ref[:, base:base + MLA_NOPE] = kv[:, h * MLA_NOPE:(h + 1) * MLA_NOPE].astype(BF16)
        kf_ref[:, base + MLA_NOPE:base + MLA_HEAD_PAD] = k_rope


def _mla_proj(proj, cs_m, q_norm_g, kv_norm_g, wq_ext, wkv):
    t = proj.shape[0]
    tiles_per_seq = SEQ // ROW_TILE
    wide = MLA_HEADS * MLA_HEAD_PAD
    row = lambda i: (i, 0)
    return pl.pallas_call(
        _mla_proj_kernel,
        grid=(t // ROW_TILE,),
        in_specs=[
            pl.BlockSpec((ROW_TILE, Q_LORA), lambda i: (i, OFF_DQ // Q_LORA)),
            pl.BlockSpec((ROW_TILE, KV_LORA), lambda i: (i, OFF_DKV // KV_LORA)),
            pl.BlockSpec((ROW_TILE, 128), lambda i: (i, OFF_KR // 128)),
            pl.BlockSpec((ROW_TILE, 128), lambda i: (i % tiles_per_seq, 0)),
            pl.BlockSpec((1, Q_LORA), lambda i: (0, 0)),
            pl.BlockSpec((1, KV_LORA), lambda i: (0, 0)),
            pl.BlockSpec((Q_LORA, wide), lambda i: (0, 0)),
            pl.BlockSpec((KV_LORA, 2 * MLA_V_W), lambda i: (0, 0)),
        ],
        out_specs=[
            pl.BlockSpec((ROW_TILE, wide), row),
            pl.BlockSpec((ROW_TILE, wide), row),
            pl.BlockSpec((ROW_TILE, MLA_V_W), row),
        ],
        out_shape=[
            jax.ShapeDtypeStruct((t, wide), BF16),
            jax.ShapeDtypeStruct((t, wide), BF16),
            jax.ShapeDtypeStruct((t, MLA_V_W), BF16),
        ],
        compiler_params=_cparams("parallel"),
        name="mla_proj",
    )(proj, proj, proj, cs_m, q_norm_g, kv_norm_g, wq_ext, wkv)


def _retention_kernel(q_ref, k_ref, v_ref, cos_ref, sin_ref, df_ref, db_ref, o_ref,
                      qs_ref, ks_ref, acc_ref):
    c = RET_CHUNK
    n_chunks = SEQ // c
    qf = q_ref[...].astype(F32)
    kf = k_ref[...].astype(F32)
    cos = cos_ref[...]
    sin = sin_ref[...]
    qs_ref[...] = (qf * cos + pltpu.roll(qf, RET_QK_DIM // 2, axis=1) * sin).astype(BF16)
    ks_ref[...] = ((kf * cos + pltpu.roll(kf, RET_QK_DIM // 2, axis=1) * sin)
                   * (RET_QK_DIM ** -0.5)).astype(BF16)

    lg_f = jnp.log(jax.nn.sigmoid(df_ref[...]))
    lg_b = jnp.log(jax.nn.sigmoid(db_ref[...]))
    lgf = lg_f[:, :c]
    lgb = lg_b[:, :c]
    row = lax.broadcasted_iota(jnp.int32, (c, c), 0).astype(F32)
    col = lax.broadcasted_iota(jnp.int32, (c, c), 1).astype(F32)
    diff = row - col
    dmat = jnp.where(diff >= 0, jnp.exp(lgf * jnp.maximum(diff, 0.0)),
                     jnp.exp(lgb * jnp.maximum(-diff, 0.0)))
    qd_f = jnp.exp(lgf * (row + 1.0))
    kd_f = jnp.exp(lgf * (c - 1.0 - row))
    qd_b = jnp.exp(lgb * (c - row))
    kd_b = jnp.exp(lgb * row)
    cd_f = jnp.exp(lg_f * float(c))
    cd_b = jnp.exp(lg_b * float(c))

    def kv_outer(kb, vb):
        return lax.dot_general(kb, vb, (((0,), (0,)), ((), ())), preferred_element_type=F32)

    def fwd_body(n, state):
        rows = pl.ds(pl.multiple_of(n * c, c), c)
        qn = qs_ref[rows, :]
        kn = ks_ref[rows, :]
        vn = v_ref[rows, :]
        s = lax.dot_general(qn, kn, (((1,), (1,)), ((), ())), preferred_element_type=F32) * dmat
        o = jnp.dot(s.astype(BF16), vn, preferred_element_type=F32)
        o += jnp.dot((qn.astype(F32) * qd_f).astype(BF16), state.astype(BF16),
                     preferred_element_type=F32)
        acc_ref[rows, :] = o
        return cd_f * state + kv_outer((kn.astype(F32) * kd_f).astype(BF16), vn)

    lax.fori_loop(0, n_chunks, fwd_body, jnp.zeros((RET_QK_DIM, RET_V_DIM), F32))

    def bwd_body(i, state):
        n = n_chunks - 1 - i
        rows = pl.ds(pl.multiple_of(n * c, c), c)
        qn = qs_ref[rows, :]
        kn = ks_ref[rows, :]
        vn = v_ref[rows, :]
        acc_ref[rows, :] += jnp.dot((qn.astype(F32) * qd_b).astype(BF16), state.astype(BF16),
                                    preferred_element_type=F32)
        return cd_b * state + kv_outer((kn.astype(F32) * kd_b).astype(BF16), vn)

    lax.fori_loop(0, n_chunks, bwd_body, jnp.zeros((RET_QK_DIM, RET_V_DIM), F32))

    o_ref[...] = _layernorm_rows(acc_ref[...]).astype(o_ref.dtype)


def _retention(proj, cos_r, sin_r, dec_f, dec_b):
    t = proj.shape[0]
    nb = t // SEQ
    tab = pl.BlockSpec((SEQ, RET_QK_DIM), lambda b, h: (0, 0))
    dec = pl.BlockSpec((None, 1, RET_V_DIM), lambda b, h: (h, 0, 0))
    return pl.pallas_call(
        _retention_kernel,
        grid=(nb, RET_HEADS),
        in_specs=[
            pl.BlockSpec((SEQ, RET_QK_DIM), lambda b, h: (b, OFF_RQ // RET_QK_DIM + h)),
            pl.BlockSpec((SEQ, RET_QK_DIM), lambda b, h: (b, OFF_RK // RET_QK_DIM + h)),
            pl.BlockSpec((SEQ, RET_V_DIM), lambda b, h: (b, OFF_RV // RET_V_DIM + h)),
            tab, tab, dec, dec,
        ],
        out_specs=pl.BlockSpec((SEQ, RET_V_DIM), lambda b, h: (b, h)),
        out_shape=jax.ShapeDtypeStruct((t, RET_V_W), BF16),
        scratch_shapes=[
            pltpu.VMEM((SEQ, RET_QK_DIM), BF16),
            pltpu.VMEM((SEQ, RET_QK_DIM), BF16),
            pltpu.VMEM((SEQ, RET_V_DIM), F32),
        ],
        compiler_params=_cparams("parallel", "parallel"),
        name="retention",
    )(proj, proj, proj, cos_r, sin_r, dec_f, dec_b)


def _attention_kernel(q_ref, k_ref, v_ref, o_ref):
    s = lax.dot_general(q_ref[...], k_ref[...], (((1,), (1,)), ((), ())),
                        preferred_element_type=F32)
    m = jnp.max(s, axis=-1, keepdims=True)
    p = jnp.exp(s - m)
    l = jnp.sum(p, axis=-1, keepdims=True)
    o = jnp.dot(p.astype(BF16), v_ref[...], preferred_element_type=F32)
    o_ref[...] = (o / l).astype(o_ref.dtype)


def _attention(qf, kf, v):
    t = qf.shape[0]
    nb = t // SEQ
    q_tiles = SEQ // ATT_Q_TILE
    return pl.pallas_call(
        _attention_kernel,
        grid=(nb, MLA_HEADS, q_tiles),
        in_specs=[
            pl.BlockSpec((ATT_Q_TILE, MLA_HEAD_PAD), lambda b, h, i: (b * q_tiles + i, h)),
            pl.BlockSpec((SEQ, MLA_HEAD_PAD), lambda b, h, i: (b, h)),
            pl.BlockSpec((SEQ, MLA_V), lambda b, h, i: (b, h)),
        ],
        out_specs=pl.BlockSpec((ATT_Q_TILE, MLA_V), lambda b, h, i: (b * q_tiles + i, h)),
        out_shape=jax.ShapeDtypeStruct((t, MLA_V_W), BF16),
        compiler_params=_cparams("parallel", "parallel", "arbitrary"),
        name="mla_attention",
    )(qf, kf, v)


def _merge_kernel(x_ref, ron_ref, rg_ref, att_ref, ga_ref, gb_ref, g1_ref, gn_ref,
                  wro_ref, wmo_ref, wout_ref, lng_ref, lnb_ref, o_ref):
    rg = rg_ref[...].astype(F32)
    ya_in = (rg * jax.nn.sigmoid(rg)) * (ron_ref[...].astype(F32) * gn_ref[...])
    y_a = jnp.dot(ya_in.astype(BF16), wro_ref[...], preferred_element_type=F32)
    y_b = jnp.dot(att_ref[...], wmo_ref[...], preferred_element_type=F32)
    merged = (jax.nn.sigmoid(ga_ref[...].astype(F32)) * y_a
              + jax.nn.sigmoid(gb_ref[...].astype(F32)) * y_b)
    f = jnp.dot(merged.astype(BF16), wout_ref[...], preferred_element_type=F32)
    z = DEEPNORM_ALPHA * x_ref[...] + (1.0 + g1_ref[...]) * f
    o_ref[...] = _layernorm_rows(z) * lng_ref[...] + lnb_ref[...]


def _merge(x, ron, proj, att, g1, gn_g, w_ro, w_mo, w_out, ln_g, ln_b):
    t = x.shape[0]
    tiles_per_seq = SEQ // ROW_TILE
    row = lambda i: (i, 0)
    full = lambda i: (0, 0)
    wide = pl.BlockSpec((ROW_TILE, D_MODEL), row)
    vec = pl.BlockSpec((1, D_MODEL), full)
    mat = pl.BlockSpec((D_MODEL, D_MODEL), full)
    return pl.pallas_call(
        _merge_kernel,
        grid=(t // ROW_TILE,),
        in_specs=[
            wide, wide,
            pl.BlockSpec((ROW_TILE, D_MODEL), lambda i: (i, OFF_RG // D_MODEL)),
            wide,
            pl.BlockSpec((ROW_TILE, D_MODEL), lambda i: (i, OFF_GA // D_MODEL)),
            pl.BlockSpec((ROW_TILE, D_MODEL), lambda i: (i, OFF_GB // D_MODEL)),
            pl.BlockSpec((None, 1, D_MODEL), lambda i: (i // tiles_per_seq, 0, 0)),
            vec, mat, mat, mat, vec, vec,
        ],
        out_specs=wide,
        out_shape=jax.ShapeDtypeStruct((t, D_MODEL), F32),
        compiler_params=_cparams("parallel"),
        name="merge_out",
    )(x, ron, proj, att, proj, proj, g1, gn_g, w_ro, w_mo, w_out, ln_g, ln_b)


def _ffn_out_kernel(x_ref, a_ref, b_ref, ap_ref, an_ref, g2_ref, cw_ref, cb_ref, wd_ref,
                    lng_ref, lnb_ref, o_ref):
    tiles_per_seq = SEQ // ROW_TILE
    i = pl.program_id(0)
    a = a_ref[...].astype(F32)
    tm = a.shape[0]
    first = (i % tiles_per_seq) == 0
    last = (i % tiles_per_seq) == tiles_per_seq - 1
    prev_row = jnp.where(first, 0.0, ap_ref[7:8, :].astype(F32))
    next_row = jnp.where(last, 0.0, an_ref[0:1, :].astype(F32))
    r = lax.broadcasted_iota(jnp.int32, a.shape, 0)
    a_up = jnp.where(r == 0, prev_row, pltpu.roll(a, 1, axis=0))
    a_dn = jnp.where(r == tm - 1, next_row, pltpu.roll(a, tm - 1, axis=0))
    cw = cw_ref[...]
    conv = a_up * cw[0:1, :] + a * cw[1:2, :] + a_dn * cw[2:3, :] + cb_ref[...]
    gelu = 0.5 * conv * (1.0 + lax.erf(conv * (1.0 / math.sqrt(2.0))))
    hmid = (gelu * b_ref[...].astype(F32)).astype(BF16)
    y = jnp.dot(hmid, wd_ref[...], preferred_element_type=F32)
    z = DEEPNORM_ALPHA * x_ref[...] + (1.0 + g2_ref[...]) * y
    o_ref[...] = _layernorm_rows(z) * lng_ref[...] + lnb_ref[...]


def _ffn_out(x, u, g2, conv_w, conv_b, w_down, ln_g, ln_b):
    t = x.shape[0]
    tiles_per_seq = SEQ // ROW_TILE
    halo_per_tile = ROW_TILE // 8
    n_halo = t // 8
    row = lambda i: (i, 0)
    full = lambda i: (0, 0)
    return pl.pallas_call(
        _ffn_out_kernel,
        grid=(t // ROW_TILE,),
        in_specs=[
            pl.BlockSpec((ROW_TILE, D_MODEL), row),
            pl.BlockSpec((ROW_TILE, D_FF), lambda i: (i, 0)),
            pl.BlockSpec((ROW_TILE, D_FF), lambda i: (i, 1)),
            pl.BlockSpec((8, D_FF), lambda i: (jnp.maximum(i * halo_per_tile - 1, 0), 0)),
            pl.BlockSpec((8, D_FF), lambda i: (jnp.minimum((i + 1) * halo_per_tile, n_halo - 1), 0)),
            pl.BlockSpec((None, 1, D_MODEL), lambda i: (i // tiles_per_seq, 0, 0)),
            pl.BlockSpec((3, D_FF), full),
            pl.BlockSpec((1, D_FF), full),
            pl.BlockSpec((D_FF, D_MODEL), full),
            pl.BlockSpec((1, D_MODEL), full),
            pl.BlockSpec((1, D_MODEL), full),
        ],
        out_specs=pl.BlockSpec((ROW_TILE, D_MODEL), row),
        out_shape=jax.ShapeDtypeStruct((t, D_MODEL), F32),
        compiler_params=_cparams("parallel"),
        name="ffn_out",
    )(x, u, u, u, u, g2, conv_w, conv_b, w_down, ln_g, ln_b)


def _rope_cos_sin(dim):
    inv = 1.0 / (ROPE_THETA ** (jnp.arange(0, dim, 2, dtype=F32) / dim))
    ang = jnp.arange(SEQ, dtype=F32)[:, None] * inv[None, :]
    return jnp.cos(ang), jnp.sin(ang)


def _swap_halves(w):
    half = w.shape[-1] // 2
    return jnp.concatenate([w[..., half:], w[..., :half]], axis=-1)


def _prep_w_in(w_in):
    rq, rk, rv, rg, dq, dkv, kr, ga, gb = jnp.split(
        w_in, [512, 1024, 2048, 3072, 3328, 3456, 3520, 4544], axis=-1)
    return jnp.concatenate([rq, rk, rv, rg, ga, gb, dq, dkv, kr, _swap_halves(kr)], axis=-1).astype(BF16)


def _prep_w_uq(w_uq):
    w = w_uq.reshape(DEPTH, Q_LORA, MLA_HEADS, MLA_QK)
    rope = w[..., MLA_NOPE:]
    w = jnp.concatenate([w[..., :MLA_NOPE], rope, _swap_halves(rope)], axis=-1)
    return w.reshape(DEPTH, Q_LORA, MLA_HEADS * MLA_HEAD_PAD).astype(BF16)


def kernel(x_prompt, x_sample, c_prompt, c_sample, w_ada, b_ada, w_in, ret_decay_fwd, ret_decay_bwd,
           ret_gn_g, w_ret_o, q_norm_g, kv_norm_g, w_uq, w_uk, w_uv, w_mla_o, w_out, ln1_g, ln1_b,
           w_up, conv_w, conv_b, w_down, ln2_g, ln2_b):
    nb_p = x_prompt.shape[0]
    x = jnp.concatenate([x_prompt, x_sample], axis=0)
    nb = x.shape[0]
    x = x.reshape(nb * SEQ, D_MODEL)
    c = jnp.concatenate([c_prompt, c_sample], axis=0)

    ada = _ada_table(c, w_ada, b_ada).reshape(DEPTH, 6, nb, 1, D_MODEL)

    cos_r, sin_r = _rope_cos_sin(RET_QK_DIM)
    cos_r = jnp.concatenate([cos_r, cos_r], axis=-1)
    sin_r = jnp.concatenate([-sin_r, sin_r], axis=-1)
    cos_m, sin_m = _rope_cos_sin(MLA_ROPE)
    cs_m = jnp.concatenate([cos_m, cos_m, -sin_m, sin_m], axis=-1)

    w_in_e = _prep_w_in(w_in)
    w_uq_e = _prep_w_uq(w_uq)
    w_ukv = jnp.concatenate([w_uk, w_uv], axis=-1).astype(BF16)
    w_ro, w_mo, w_o = w_ret_o.astype(BF16), w_mla_o.astype(BF16), w_out.astype(BF16)
    w_u, w_d = w_up.astype(BF16), w_down.astype(BF16)
    dec_f = jnp.broadcast_to(ret_decay_fwd[:, :, None, None], (DEPTH, RET_HEADS, 1, RET_V_DIM))
    dec_b = jnp.broadcast_to(ret_decay_bwd[:, :, None, None], (DEPTH, RET_HEADS, 1, RET_V_DIM))

    for l in range(DEPTH):
        sh1, sc1, g1, sh2, sc2, g2 = (ada[l, k] for k in range(6))
        proj = _ln_mod_mm(x, sc1, sh1, w_in_e[l], "in_proj")
        qf, kf, v = _mla_proj(proj, cs_m, q_norm_g[l][None], kv_norm_g[l][None], w_uq_e[l], w_ukv[l])
        ron = _retention(proj, cos_r, sin_r, dec_f[l], dec_b[l])
        att = _attention(qf, kf, v)
        x = _merge(x, ron, proj, att, g1, ret_gn_g[l][None], w_ro[l], w_mo[l], w_o[l],
                   ln1_g[l][None], ln1_b[l][None])
        u = _ln_mod_mm(x, sc2, sh2, w_u[l], "up_proj")
        x = _ffn_out(x, u, g2, conv_w[l], conv_b[l][None], w_d[l], ln2_g[l][None], ln2_b[l][None])

    x = x.reshape(nb, SEQ, D_MODEL)
    return (x[:nb_p], x[nb_p:])
```

```python
import functools
import math

import jax
import jax.numpy as jnp
from jax import lax
from jax.experimental import pallas as pl
from jax.experimental.pallas import tpu as pltpu

D_MODEL = 1024
SEQ = 2048
DEPTH = 4
RET_HEADS = 4
RET_QK_DIM = 128
RET_V_DIM = 256
RET_QK_W = RET_HEADS * RET_QK_DIM
RET_V_W = RET_HEADS * RET_V_DIM
RET_CHUNK = 128
MLA_HEADS = 8
MLA_NOPE = 128
MLA_ROPE = 64
MLA_V = 128
Q_LORA = 256
KV_LORA = 128
MLA_QK = MLA_NOPE + MLA_ROPE
MLA_V_W = MLA_HEADS * MLA_V
MLA_HEAD_PAD = 256
D_FF = 2816
ROPE_THETA = 10000.0
LN_EPS = 1e-5
RMS_EPS = 1e-6
DEEPNORM_ALPHA = (2.0 * DEPTH) ** 0.25

IN_EXT_W = 5632
OFF_RQ, OFF_RK, OFF_RV, OFF_RG, OFF_GA, OFF_GB = 0, 512, 1024, 2048, 3072, 4096
OFF_DQ, OFF_DKV, OFF_KR = 5120, 5376, 5504

V7X_VMEM_LIMIT_BYTES = 56 * 1024 * 1024
ROW_TILE = 512
MM_COL_TILE = 512
ATT_Q_TILE = 1024
ATT_SUB_TILE = 512
LOG2_E = 1.4426950408889634

BF16 = jnp.bfloat16
F32 = jnp.float32


def _cparams(*sem):
    return pltpu.CompilerParams(dimension_semantics=sem, vmem_limit_bytes=V7X_VMEM_LIMIT_BYTES)


def _layernorm_rows(x):
    mu = jnp.mean(x, axis=-1, keepdims=True)
    xc = x - mu
    var = jnp.mean(xc * xc, axis=-1, keepdims=True)
    return xc * lax.rsqrt(var + LN_EPS)


def _ada_kernel(c_ref, w_ref, b_ref, o_ref):
    c = c_ref[...]
    s = (c * jax.nn.sigmoid(c)).astype(BF16)
    o_ref[...] = jnp.dot(s, w_ref[...].astype(BF16), preferred_element_type=F32) + b_ref[...]


def _ada_table(c, w_ada, b_ada):
    nb = c.shape[0]
    return pl.pallas_call(
        _ada_kernel,
        grid=(DEPTH, 6),
        in_specs=[
            pl.BlockSpec((nb, D_MODEL), lambda l, j: (0, 0)),
            pl.BlockSpec((None, D_MODEL, D_MODEL), lambda l, j: (l, 0, j)),
            pl.BlockSpec((None, 1, D_MODEL), lambda l, j: (l, 0, j)),
        ],
        out_specs=pl.BlockSpec((None, None, nb, D_MODEL), lambda l, j: (l, j, 0, 0)),
        out_shape=jax.ShapeDtypeStruct((DEPTH, 6, nb, D_MODEL), F32),
        compiler_params=_cparams("parallel", "parallel"),
        name="ada_table",
    )(c, w_ada, b_ada.reshape(DEPTH, 1, 6 * D_MODEL))


def _ln_mod_mm_kernel(x_ref, sc_ref, sh_ref, w_ref, o_ref):
    h = _layernorm_rows(x_ref[...]) * (1.0 + sc_ref[...]) + sh_ref[...]
    hb = h.astype(BF16)
    n_out = o_ref.shape[1]
    for n in range(n_out // MM_COL_TILE):
        cols = slice(n * MM_COL_TILE, (n + 1) * MM_COL_TILE)
        o_ref[:, cols] = jnp.dot(hb, w_ref[:, cols], preferred_element_type=F32).astype(o_ref.dtype)


def _ln_mod_mm(x, sc, sh, w, name):
    t = x.shape[0]
    n_out = w.shape[1]
    tiles_per_seq = SEQ // ROW_TILE
    mod_spec = pl.BlockSpec((None, 1, D_MODEL), lambda i: (i // tiles_per_seq, 0, 0))
    return pl.pallas_call(
        _ln_mod_mm_kernel,
        grid=(t // ROW_TILE,),
        in_specs=[
            pl.BlockSpec((ROW_TILE, D_MODEL), lambda i: (i, 0)),
            mod_spec, mod_spec,
            pl.BlockSpec((D_MODEL, n_out), lambda i: (0, 0)),
        ],
        out_specs=pl.BlockSpec((ROW_TILE, n_out), lambda i: (i, 0)),
        out_shape=jax.ShapeDtypeStruct((t, n_out), BF16),
        compiler_params=_cparams("parallel"),
        name=name,
    )(x, sc, sh, w)


def _rms_rows(x, g):
    return x * lax.rsqrt(jnp.mean(x * x, axis=-1, keepdims=True) + RMS_EPS) * g


def _mla_proj_kernel(dq_ref, dkv_ref, kr_ref, cs_ref, qg_ref, kvg_ref, wq_ref, wkv_ref,
                     qf_ref, kf_ref, vt_ref):
    scale = MLA_QK ** -0.5 * LOG2_E
    cs = cs_ref[...]
    cq = _rms_rows(dq_ref[...].astype(F32), qg_ref[...]).astype(BF16)
    ckv = _rms_rows(dkv_ref[...].astype(F32), kvg_ref[...]).astype(BF16)
    tk = kr_ref[...].astype(F32) * cs
    lane = lax.broadcasted_iota(jnp.int32, tk.shape, 1)
    k_rope = jnp.where(lane < MLA_ROPE, tk + pltpu.roll(tk, MLA_ROPE, axis=1), 0.0).astype(BF16)
    kv = jnp.dot(ckv, wkv_ref[...], preferred_element_type=F32)
    vt_ref[...] = kv[:, MLA_V_W:].T.astype(BF16)
    for h in range(MLA_HEADS):
        base = h * MLA_HEAD_PAD
        qh = jnp.dot(cq, wq_ref[:, base:base + MLA_HEAD_PAD], preferred_element_type=F32)
        tq = qh[:, MLA_NOPE:] * cs
        q_rope = tq + pltpu.roll(tq, MLA_ROPE, axis=1)
        qf_ref[:, base:base + MLA_NOPE] = (qh[:, :MLA_NOPE] * scale).astype(BF16)
        qf_ref[:, base + MLA_NOPE:base + MLA_HEAD_PAD] = (q_rope * scale).astype(BF16)
        kf_ref[:, base:base + MLA_NOPE] = kv[:, h * MLA_NOPE:(h + 1) * MLA_NOPE].astype(BF16)
        kf_ref[:, base + MLA_NOPE:base + MLA_HEAD_PAD] = k_rope


def _mla_proj(proj, cs_m, q_norm_g, kv_norm_g, wq_ext, wkv):
    t = proj.shape[0]
    tiles_per_seq = SEQ // ROW_TILE
    wide = MLA_HEADS * MLA_HEAD_PAD
    row = lambda i: (i, 0)
    return pl.pallas_call(
        _mla_proj_kernel,
        grid=(t // ROW_TILE,),
        in_specs=[
            pl.BlockSpec((ROW_TILE, Q_LORA), lambda i: (i, OFF_DQ // Q_LORA)),
            pl.BlockSpec((ROW_TILE, KV_LORA), lambda i: (i, OFF_DKV // KV_LORA)),
            pl.BlockSpec((ROW_TILE, 128), lambda i: (i, OFF_KR // 128)),
            pl.BlockSpec((ROW_TILE, 128), lambda i: (i % tiles_per_seq, 0)),
            pl.BlockSpec((1, Q_LORA), lambda i: (0, 0)),
            pl.BlockSpec((1, KV_LORA), lambda i: (0, 0)),
            pl.BlockSpec((Q_LORA, wide), lambda i: (0, 0)),
            pl.BlockSpec((KV_LORA, 2 * MLA_V_W), lambda i: (0, 0)),
        ],
        out_specs=[
            pl.BlockSpec((ROW_TILE, wide), row),
            pl.BlockSpec((ROW_TILE, wide), row),
            pl.BlockSpec((None, MLA_V_W, ROW_TILE), lambda i: (i // tiles_per_seq, 0, i % tiles_per_seq)),
        ],
        out_shape=[
            jax.ShapeDtypeStruct((t, wide), BF16),
            jax.ShapeDtypeStruct((t, wide), BF16),
            jax.ShapeDtypeStruct((t // SEQ, MLA_V_W, SEQ), BF16),
        ],
        compiler_params=_cparams("parallel"),
        name="mla_proj",
    )(proj, proj, proj, cs_m, q_norm_g, kv_norm_g, wq_ext, wkv)


def _retention_kernel(q_ref, k_ref, v_ref, cos_ref, sin_ref, df_ref, db_ref, o_ref,
                      lhs_ref, rhs_ref, kvf_ref, kvb_ref):
    c = RET_CHUNK
    n_chunks = SEQ // c
    dk = RET_QK_DIM
    qf = q_ref[...].astype(F32)
    kf = k_ref[...].astype(F32)
    cos = cos_ref[...]
    sin = sin_ref[...]
    q = qf * cos + pltpu.roll(qf, dk // 2, axis=1) * sin
    k = (kf * cos + pltpu.roll(kf, dk // 2, axis=1) * sin) * (dk ** -0.5)
    kt = k.T

    lg_f = jnp.log(jax.nn.sigmoid(df_ref[...]))
    lg_b = jnp.log(jax.nn.sigmoid(db_ref[...]))
    lgf = lg_f[:, :c]
    lgb = lg_b[:, :c]
    row = lax.broadcasted_iota(jnp.int32, (c, c), 0).astype(F32)
    col = lax.broadcasted_iota(jnp.int32, (c, c), 1).astype(F32)
    diff = row - col
    dmat = jnp.where(diff >= 0, jnp.exp(lgf * jnp.maximum(diff, 0.0)),
                     jnp.exp(lgb * jnp.maximum(-diff, 0.0)))
    qd_f = jnp.exp(lgf * (row + 1.0))
    qd_b = jnp.exp(lgb * (c - row))
    pos = col[0:1, :]
    kd_f = jnp.exp(lgf * (c - 1.0 - pos))
    kd_b = jnp.exp(lgb * pos)
    cd_f = jnp.exp(lg_f * float(c))
    cd_b = jnp.exp(lg_b * float(c))

    for n in range(n_chunks):
        rows = slice(n * c, (n + 1) * c)
        qn = q[rows, :]
        ktn = kt[:, rows]
        vn = v_ref[rows, :]
        s = jnp.dot(qn.astype(BF16), ktn.astype(BF16), preferred_element_type=F32) * dmat
        lhs_ref[rows, 0:c] = s.astype(BF16)
        lhs_ref[rows, c:c + dk] = (qn * qd_f).astype(BF16)
        lhs_ref[rows, c + dk:c + 2 * dk] = (qn * qd_b).astype(BF16)
        rhs_ref[n, 0:c, :] = vn
        kvf_ref[n] = jnp.dot((ktn * kd_f).astype(BF16), vn, preferred_element_type=F32)
        kvb_ref[n] = jnp.dot((ktn * kd_b).astype(BF16), vn, preferred_element_type=F32)

    state = jnp.zeros((dk, RET_V_DIM), F32)
    for n in range(n_chunks):
        rhs_ref[n, c:c + dk, :] = state.astype(BF16)
        state = cd_f * state + kvf_ref[n]
    state = jnp.zeros((dk, RET_V_DIM), F32)
    for n in reversed(range(n_chunks)):
        rhs_ref[n, c + dk:c + 2 * dk, :] = state.astype(BF16)
        state = cd_b * state + kvb_ref[n]

    for n in range(n_chunks):
        rows = slice(n * c, (n + 1) * c)
        o = jnp.dot(lhs_ref[rows, :], rhs_ref[n], preferred_element_type=F32)
        o_ref[rows, :] = _layernorm_rows(o).astype(o_ref.dtype)


def _retention(proj, cos_r, sin_r, dec_f, dec_b):
    t = proj.shape[0]
    nb = t // SEQ
    tab = pl.BlockSpec((SEQ, RET_QK_DIM), lambda b, h: (0, 0))
    dec = pl.BlockSpec((None, 1, RET_V_DIM), lambda b, h: (h, 0, 0))
    return pl.pallas_call(
        _retention_kernel,
        grid=(nb, RET_HEADS),
        in_specs=[
            pl.BlockSpec((SEQ, RET_QK_DIM), lambda b, h: (b, OFF_RQ // RET_QK_DIM + h)),
            pl.BlockSpec((SEQ, RET_QK_DIM), lambda b, h: (b, OFF_RK // RET_QK_DIM + h)),
            pl.BlockSpec((SEQ, RET_V_DIM), lambda b, h: (b, OFF_RV // RET_V_DIM + h)),
            tab, tab, dec, dec,
        ],
        out_specs=pl.BlockSpec((SEQ, RET_V_DIM), lambda b, h: (b, h)),
        out_shape=jax.ShapeDtypeStruct((t, RET_V_W), BF16),
        scratch_shapes=[
            pltpu.VMEM((SEQ, RET_CHUNK + 2 * RET_QK_DIM), BF16),
            pltpu.VMEM((SEQ // RET_CHUNK, RET_CHUNK + 2 * RET_QK_DIM, RET_V_DIM), BF16),
            pltpu.VMEM((SEQ // RET_CHUNK, RET_QK_DIM, RET_V_DIM), F32),
            pltpu.VMEM((SEQ // RET_CHUNK, RET_QK_DIM, RET_V_DIM), F32),
        ],
        compiler_params=_cparams("parallel", "parallel"),
        name="retention",
    )(proj, proj, proj, cos_r, sin_r, dec_f, dec_b)


def _attention_kernel(q_ref, k_ref, vt_ref, o_ref):
    k = k_ref[...]
    vt = vt_ref[...]
    for j in range(ATT_Q_TILE // ATT_SUB_TILE):
        rows = slice(j * ATT_SUB_TILE, (j + 1) * ATT_SUB_TILE)
        st = lax.dot_general(k, q_ref[rows, :], (((1,), (1,)), ((), ())),
                             preferred_element_type=F32)
        m = jnp.max(st, axis=0, keepdims=True)
        p = jnp.exp2(st - m)
        l = jnp.sum(p, axis=0, keepdims=True)
        ot = jnp.dot(vt, p.astype(BF16), preferred_element_type=F32)
        o_ref[rows, :] = (ot / l).T.astype(o_ref.dtype)


def _attention(qf, kf, vt):
    t = qf.shape[0]
    nb = t // SEQ
    q_tiles = SEQ // ATT_Q_TILE
    return pl.pallas_call(
        _attention_kernel,
        grid=(nb, MLA_HEADS, q_tiles),
        in_specs=[
            pl.BlockSpec((ATT_Q_TILE, MLA_HEAD_PAD), lambda b, h, i: (b * q_tiles + i, h)),
            pl.BlockSpec((SEQ, MLA_HEAD_PAD), lambda b, h, i: (b, h)),
            pl.BlockSpec((None, MLA_V, SEQ), lambda b, h, i: (b, h, 0)),
        ],
        out_specs=pl.BlockSpec((ATT_Q_TILE, MLA_V), lambda b, h, i: (b * q_tiles + i, h)),
        out_shape=jax.ShapeDtypeStruct((t, MLA_V_W), BF16),
        compiler_params=_cparams("parallel", "parallel", "arbitrary"),
        name="mla_attention",
    )(qf, kf, vt)


def _merge_kernel(x_ref, ron_ref, rg_ref, att_ref, ga_ref, gb_ref, g1_ref, gn_ref,
                  wro_ref, wmo_ref, wout_ref, lng_ref, lnb_ref, o_ref):
    rg = rg_ref[...].astype(F32)
    ya_in = (rg * jax.nn.sigmoid(rg)) * (ron_ref[...].astype(F32) * gn_ref[...])
    y_a = jnp.dot(ya_in.astype(BF16), wro_ref[...], preferred_element_type=F32)
    y_b = jnp.dot(att_ref[...], wmo_ref[...], preferred_element_type=F32)
    merged = (jax.nn.sigmoid(ga_ref[...].astype(F32)) * y_a
              + jax.nn.sigmoid(gb_ref[...].astype(F32)) * y_b)
    f = jnp.dot(merged.astype(BF16), wout_ref[...], preferred_element_type=F32)
    z = DEEPNORM_ALPHA * x_ref[...] + (1.0 + g1_ref[...]) * f
    o_ref[...] = _layernorm_rows(z) * lng_ref[...] + lnb_ref[...]


def _merge(x, ron, proj, att, g1, gn_g, w_ro, w_mo, w_out, ln_g, ln_b):
    t = x.shape[0]
    tiles_per_seq = SEQ // ROW_TILE
    row = lambda i: (i, 0)
    full = lambda i: (0, 0)
    wide = pl.BlockSpec((ROW_TILE, D_MODEL), row)
    vec = pl.BlockSpec((1, D_MODEL), full)
    mat = pl.BlockSpec((D_MODEL, D_MODEL), full)
    return pl.pallas_call(
        _merge_kernel,
        grid=(t // ROW_TILE,),
        in_specs=[
            wide, wide,
            pl.BlockSpec((ROW_TILE, D_MODEL), lambda i: (i, OFF_RG // D_MODEL)),
            wide,
            pl.BlockSpec((ROW_TILE, D_MODEL), lambda i: (i, OFF_GA // D_MODEL)),
            pl.BlockSpec((ROW_TILE, D_MODEL), lambda i: (i, OFF_GB // D_MODEL)),
            pl.BlockSpec((None, 1, D_MODEL), lambda i: (i // tiles_per_seq, 0, 0)),
            vec, mat, mat, mat, vec, vec,
        ],
        out_specs=wide,
        out_shape=jax.ShapeDtypeStruct((t, D_MODEL), F32),
        compiler_params=_cparams("parallel"),
        name="merge_out",
    )(x, ron, proj, att, proj, proj, g1, gn_g, w_ro, w_mo, w_out, ln_g, ln_b)


def _ffn_out_kernel(x_ref, a_ref, b_ref, ap_ref, an_ref, g2_ref, cw_ref, cb_ref, wd_ref,
                    lng_ref, lnb_ref, o_ref):
    tiles_per_seq = SEQ // ROW_TILE
    i = pl.program_id(0)
    a = a_ref[...].astype(F32)
    tm = a.shape[0]
    first = (i % tiles_per_seq) == 0
    last = (i % tiles_per_seq) == tiles_per_seq - 1
    prev_row = jnp.where(first, 0.0, ap_ref[7:8, :].astype(F32))
    next_row = jnp.where(last, 0.0, an_ref[0:1, :].astype(F32))
    r = lax.broadcasted_iota(jnp.int32, a.shape, 0)
    a_up = jnp.where(r == 0, prev_row, pltpu.roll(a, 1, axis=0))
    a_dn = jnp.where(r == tm - 1, next_row, pltpu.roll(a, tm - 1, axis=0))
    cw = cw_ref[...]
    conv = a_up * cw[0:1, :] + a * cw[1:2, :] + a_dn * cw[2:3, :] + cb_ref[...]
    gelu = 0.5 * conv * (1.0 + lax.erf(conv * (1.0 / math.sqrt(2.0))))
    hmid = (gelu * b_ref[...].astype(F32)).astype(BF16)
    y = jnp.dot(hmid, wd_ref[...], preferred_element_type=F32)
    z = DEEPNORM_ALPHA * x_ref[...] + (1.0 + g2_ref[...]) * y
    o_ref[...] = _layernorm_rows(z) * lng_ref[...] + lnb_ref[...]


def _ffn_out(x, u, g2, conv_w, conv_b, w_down, ln_g, ln_b):
    t = x.shape[0]
    tiles_per_seq = SEQ // ROW_TILE
    halo_per_tile = ROW_TILE // 8
    n_halo = t // 8
    row = lambda i: (i, 0)
    full = lambda i: (0, 0)
    return pl.pallas_call(
        _ffn_out_kernel,
        grid=(t // ROW_TILE,),
        in_specs=[
            pl.BlockSpec((ROW_TILE, D_MODEL), row),
            pl.BlockSpec((ROW_TILE, D_FF), lambda i: (i, 0)),
            pl.BlockSpec((ROW_TILE, D_FF), lambda i: (i, 1)),
            pl.BlockSpec((8, D_FF), lambda i: (jnp.maximum(i * halo_per_tile - 1, 0), 0)),
            pl.BlockSpec((8, D_FF), lambda i: (jnp.minimum((i + 1) * halo_per_tile, n_halo - 1), 0)),
            pl.BlockSpec((None, 1, D_MODEL), lambda i: (i // tiles_per_seq, 0, 0)),
            pl.BlockSpec((3, D_FF), full),
            pl.BlockSpec((1, D_FF), full),
            pl.BlockSpec((D_FF, D_MODEL), full),
            pl.BlockSpec((1, D_MODEL), full),
            pl.BlockSpec((1, D_MODEL), full),
        ],
        out_specs=pl.BlockSpec((ROW_TILE, D_MODEL), row),
        out_shape=jax.ShapeDtypeStruct((t, D_MODEL), F32),
        compiler_params=_cparams("parallel"),
        name="ffn_out",
    )(x, u, u, u, u, g2, conv_w, conv_b, w_down, ln_g, ln_b)


def _rope_cos_sin(dim):
    inv = 1.0 / (ROPE_THETA ** (jnp.arange(0, dim, 2, dtype=F32) / dim))
    ang = jnp.arange(SEQ, dtype=F32)[:, None] * inv[None, :]
    return jnp.cos(ang), jnp.sin(ang)


def _swap_halves(w):
    half = w.shape[-1] // 2
    return jnp.concatenate([w[..., half:], w[..., :half]], axis=-1)


def _prep_w_in(w_in):
    rq, rk, rv, rg, dq, dkv, kr, ga, gb = jnp.split(
        w_in, [512, 1024, 2048, 3072, 3328, 3456, 3520, 4544], axis=-1)
    return jnp.concatenate([rq, rk, rv, rg, ga, gb, dq, dkv, kr, _swap_halves(kr)], axis=-1).astype(BF16)


def _prep_w_uq(w_uq):
    w = w_uq.reshape(DEPTH, Q_LORA, MLA_HEADS, MLA_QK)
    rope = w[..., MLA_NOPE:]
    w = jnp.concatenate([w[..., :MLA_NOPE], rope, _swap_halves(rope)], axis=-1)
    return w.reshape(DEPTH, Q_LORA, MLA_HEADS * MLA_HEAD_PAD).astype(BF16)


def kernel(x_prompt, x_sample, c_prompt, c_sample, w_ada, b_ada, w_in, ret_decay_fwd, ret_decay_bwd,
           ret_gn_g, w_ret_o, q_norm_g, kv_norm_g, w_uq, w_uk, w_uv, w_mla_o, w_out, ln1_g, ln1_b,
           w_up, conv_w, conv_b, w_down, ln2_g, ln2_b):
    nb_p = x_prompt.shape[0]
    x = jnp.concatenate([x_prompt, x_sample], axis=0)
    nb = x.shape[0]
    x = x.reshape(nb * SEQ, D_MODEL)
    c = jnp.concatenate([c_prompt, c_sample], axis=0)

    ada = _ada_table(c, w_ada, b_ada).reshape(DEPTH, 6, nb, 1, D_MODEL)

    cos_r, sin_r = _rope_cos_sin(RET_QK_DIM)
    cos_r = jnp.concatenate([cos_r, cos_r], axis=-1)
    sin_r = jnp.concatenate([-sin_r, sin_r], axis=-1)
    cos_m, sin_m = _rope_cos_sin(MLA_ROPE)
    cs_m = jnp.concatenate([cos_m, cos_m, -sin_m, sin_m], axis=-1)

    w_in_e = _prep_w_in(w_in)
    w_uq_e = _prep_w_uq(w_uq)
    w_ukv = jnp.concatenate([w_uk, w_uv], axis=-1).astype(BF16)
    w_ro, w_mo, w_o = w_ret_o.astype(BF16), w_mla_o.astype(BF16), w_out.astype(BF16)
    w_u, w_d = w_up.astype(BF16), w_down.astype(BF16)
    dec_f = jnp.broadcast_to(ret_decay_fwd[:, :, None, None], (DEPTH, RET_HEADS, 1, RET_V_DIM))
    dec_b = jnp.broadcast_to(ret_decay_bwd[:, :, None, None], (DEPTH, RET_HEADS, 1, RET_V_DIM))

    for l in range(DEPTH):
        sh1, sc1, g1, sh2, sc2, g2 = (ada[l, k] for k in range(6))
        proj = _ln_mod_mm(x, sc1, sh1, w_in_e[l], "in_proj")
        qf, kf, vt = _mla_proj(proj, cs_m, q_norm_g[l][None], kv_norm_g[l][None], w_uq_e[l], w_ukv[l])
        ron = _retention(proj, cos_r, sin_r, dec_f[l], dec_b[l])
        att = _attention(qf, kf, vt)
        x = _merge(x, ron, proj, att, g1, ret_gn_g[l][None], w_ro[l], w_mo[l], w_o[l],
                   ln1_g[l][None], ln1_b[l][None])
        u = _ln_mod_mm(x, sc2, sh2, w_u[l], "up_proj")
        x = _ffn_out(x, u, g2, conv_w[l], conv_b[l][None], w_d[l], ln2_g[l][None], ln2_b[l][None])

    x = x.reshape(nb, SEQ, D_MODEL)
    return (x[:nb_p], x[nb_p:])
```

```python
import functools
import math

import jax
import jax.numpy as jnp
from jax import lax
from jax.experimental import pallas as pl
from jax.experimental.pallas import tpu as pltpu

D_MODEL = 1024
SEQ = 2048
DEPTH = 4
RET_HEADS = 4
RET_QK_DIM = 128
RET_V_DIM = 256
RET_QK_W = RET_HEADS * RET_QK_DIM
RET_V_W = RET_HEADS * RET_V_DIM
RET_CHUNK = 128
MLA_HEADS = 8
MLA_NOPE = 128
MLA_ROPE = 64
MLA_V = 128
Q_LORA = 256
KV_LORA = 128
MLA_QK = MLA_NOPE + MLA_ROPE
MLA_V_W = MLA_HEADS * MLA_V
MLA_HEAD_PAD = 256
D_FF = 2816
ROPE_THETA = 10000.0
LN_EPS = 1e-5
RMS_EPS = 1e-6
DEEPNORM_ALPHA = (2.0 * DEPTH) ** 0.25

IN_EXT_W = 5632
OFF_RQ, OFF_RK, OFF_RV, OFF_RG, OFF_GA, OFF_GB = 0, 512, 1024, 2048, 3072, 4096
OFF_DQ, OFF_DKV, OFF_KR = 5120, 5376, 5504

V7X_VMEM_LIMIT_BYTES = 56 * 1024 * 1024
ROW_TILE = 512
MM_COL_TILE = 512
FF_COL_TILE = 256
ATT_Q_TILE = 2048
ATT_SUB_TILE = 512
LOG2_E = 1.4426950408889634

BF16 = jnp.bfloat16
F32 = jnp.float32


def _cparams(*sem):
    return pltpu.CompilerParams(dimension_semantics=sem, vmem_limit_bytes=V7X_VMEM_LIMIT_BYTES)


def _layernorm_rows(x):
    mu = jnp.mean(x, axis=-1, keepdims=True)
    xc = x - mu
    var = jnp.mean(xc * xc, axis=-1, keepdims=True)
    return xc * lax.rsqrt(var + LN_EPS)


def _ada_kernel(c_ref, w_ref, b_ref, o_ref):
    c = c_ref[...]
    s = (c * jax.nn.sigmoid(c)).astype(BF16)
    o_ref[...] = jnp.dot(s, w_ref[...].astype(BF16), preferred_element_type=F32) + b_ref[...]


def _ada_table(c, w_ada, b_ada):
    nb = c.shape[0]
    return pl.pallas_call(
        _ada_kernel,
        grid=(DEPTH, 6),
        in_specs=[
            pl.BlockSpec((nb, D_MODEL), lambda l, j: (0, 0)),
            pl.BlockSpec((None, D_MODEL, D_MODEL), lambda l, j: (l, 0, j)),
            pl.BlockSpec((None, 1, D_MODEL), lambda l, j: (l, 0, j)),
        ],
        out_specs=pl.BlockSpec((None, None, nb, D_MODEL), lambda l, j: (l, j, 0, 0)),
        out_shape=jax.ShapeDtypeStruct((DEPTH, 6, nb, D_MODEL), F32),
        compiler_params=_cparams("parallel", "parallel"),
        name="ada_table",
    )(c, w_ada, b_ada.reshape(DEPTH, 1, 6 * D_MODEL))


def _ln_mod_mm_kernel(x_ref, sc_ref, sh_ref, w_ref, o_ref):
    h = _layernorm_rows(x_ref[...]) * (1.0 + sc_ref[...]) + sh_ref[...]
    hb = h.astype(BF16)
    n_out = o_ref.shape[1]
    for n in range(n_out // MM_COL_TILE):
        cols = slice(n * MM_COL_TILE, (n + 1) * MM_COL_TILE)
        o_ref[:, cols] = jnp.dot(hb, w_ref[:, cols], preferred_element_type=F32).astype(o_ref.dtype)


def _ln_mod_mm(x, sc, sh, w, name):
    t = x.shape[0]
    n_out = w.shape[1]
    tiles_per_seq = SEQ // ROW_TILE
    mod_spec = pl.BlockSpec((None, 1, D_MODEL), lambda i: (i // tiles_per_seq, 0, 0))
    return pl.pallas_call(
        _ln_mod_mm_kernel,
        grid=(t // ROW_TILE,),
        in_specs=[
            pl.BlockSpec((ROW_TILE, D_MODEL), lambda i: (i, 0)),
            mod_spec, mod_spec,
            pl.BlockSpec((D_MODEL, n_out), lambda i: (0, 0)),
        ],
        out_specs=pl.BlockSpec((ROW_TILE, n_out), lambda i: (i, 0)),
        out_shape=jax.ShapeDtypeStruct((t, n_out), BF16),
        compiler_params=_cparams("parallel"),
        name=name,
    )(x, sc, sh, w)


def _rms_rows(x, g):
    return x * lax.rsqrt(jnp.mean(x * x, axis=-1, keepdims=True) + RMS_EPS) * g


def _mla_proj_kernel(dq_ref, dkv_ref, kr_ref, cs_ref, qg_ref, kvg_ref, wq_ref, wkv_ref,
                     qf_ref, kf_ref, vt_ref):
    scale = MLA_QK ** -0.5 * LOG2_E
    cs = cs_ref[...]
    cq = _rms_rows(dq_ref[...].astype(F32), qg_ref[...]).astype(BF16)
    ckv = _rms_rows(dkv_ref[...].astype(F32), kvg_ref[...]).astype(BF16)
    tk = kr_ref[...].astype(F32) * cs
    lane = lax.broadcasted_iota(jnp.int32, tk.shape, 1)
    k_rope = jnp.where(lane < MLA_ROPE, tk + pltpu.roll(tk, MLA_ROPE, axis=1), 0.0).astype(BF16)
    kv = jnp.dot(ckv, wkv_ref[...], preferred_element_type=F32)
    vt_ref[...] = kv[:, MLA_V_W:].T.astype(BF16)
    for h in range(MLA_HEADS):
        base = h * MLA_HEAD_PAD
        qh = jnp.dot(cq, wq_ref[:, base:base + MLA_HEAD_PAD], preferred_element_type=F32)
        tq = qh[:, MLA_NOPE:] * cs
        q_rope = tq + pltpu.roll(tq, MLA_ROPE, axis=1)
        qf_ref[:, base:base + MLA_NOPE] = (qh[:, :MLA_NOPE] * scale).astype(BF16)
        qf_ref[:, base + MLA_NOPE:base + MLA_HEAD_PAD] = (q_rope * scale).astype(BF16)
        kf_ref[:, base:base + MLA_NOPE] = kv[:, h * MLA_NOPE:(h + 1) * MLA_NOPE].astype(BF16)
        kf_ref[:, base + MLA_NOPE:base + MLA_HEAD_PAD] = k_rope


def _mla_proj(proj, cs_m, q_norm_g, kv_norm_g, wq_ext, wkv):
    t = proj.shape[0]
    tiles_per_seq = SEQ // ROW_TILE
    wide = MLA_HEADS * MLA_HEAD_PAD
    row = lambda i: (i, 0)
    return pl.pallas_call(
        _mla_proj_kernel,
        grid=(t // ROW_TILE,),
        in_specs=[
            pl.BlockSpec((ROW_TILE, Q_LORA), lambda i: (i, OFF_DQ // Q_LORA)),
            pl.BlockSpec((ROW_TILE, KV_LORA), lambda i: (i, OFF_DKV // KV_LORA)),
            pl.BlockSpec((ROW_TILE, 128), lambda i: (i, OFF_KR // 128)),
            pl.BlockSpec((ROW_TILE, 128), lambda i: (i % tiles_per_seq, 0)),
            pl.BlockSpec((1, Q_LORA), lambda i: (0, 0)),
            pl.BlockSpec((1, KV_LORA), lambda i: (0, 0)),
            pl.BlockSpec((Q_LORA, wide), lambda i: (0, 0)),
            pl.BlockSpec((KV_LORA, 2 * MLA_V_W), lambda i: (0, 0)),
        ],
        out_specs=[
            pl.BlockSpec((ROW_TILE, wide), row),
            pl.BlockSpec((ROW_TILE, wide), row),
            pl.BlockSpec((None, MLA_V_W, ROW_TILE), lambda i: (i // tiles_per_seq, 0, i % tiles_per_seq)),
        ],
        out_shape=[
            jax.ShapeDtypeStruct((t, wide), BF16),
            jax.ShapeDtypeStruct((t, wide), BF16),
            jax.ShapeDtypeStruct((t // SEQ, MLA_V_W, SEQ), BF16),
        ],
        compiler_params=_cparams("parallel"),
        name="mla_proj",
    )(proj, proj, proj, cs_m, q_norm_g, kv_norm_g, wq_ext, wkv)


def _retention_kernel(q_ref, k_ref, v_ref, cos_ref, sin_ref, df_ref, db_ref, o_ref,
                      lhs_ref, rhs_ref, kvf_ref, kvb_ref):
    c = RET_CHUNK
    n_chunks = SEQ // c
    dk = RET_QK_DIM
    qf = q_ref[...].astype(F32)
    kf = k_ref[...].astype(F32)
    cos = cos_ref[...]
    sin = sin_ref[...]
    q = qf * cos + pltpu.roll(qf, dk // 2, axis=1) * sin
    k = (kf * cos + pltpu.roll(kf, dk // 2, axis=1) * sin) * (dk ** -0.5)
    kt = k.T

    lg_f = jnp.log(jax.nn.sigmoid(df_ref[...]))
    lg_b = jnp.log(jax.nn.sigmoid(db_ref[...]))
    lgf = lg_f[:, :c]
    lgb = lg_b[:, :c]
    row = lax.broadcasted_iota(jnp.int32, (c, c), 0).astype(F32)
    col = lax.broadcasted_iota(jnp.int32, (c, c), 1).astype(F32)
    diff = row - col
    dmat = jnp.where(diff >= 0, jnp.exp(lgf * jnp.maximum(diff, 0.0)),
                     jnp.exp(lgb * jnp.maximum(-diff, 0.0)))
    qd_f = jnp.exp(lgf * (row + 1.0))
    qd_b = jnp.exp(lgb * (c - row))
    pos = col[0:1, :]
    kd_f = jnp.exp(lgf * (c - 1.0 - pos))
    kd_b = jnp.exp(lgb * pos)
    cd_f = jnp.exp(lg_f * float(c))
    cd_b = jnp.exp(lg_b * float(c))

    for n in range(n_chunks):
        rows = slice(n * c, (n + 1) * c)
        qn = q[rows, :]
        ktn = kt[:, rows]
        vn = v_ref[rows, :]
        s = jnp.dot(qn.astype(BF16), ktn.astype(BF16), preferred_element_type=F32) * dmat
        lhs_ref[rows, 0:c] = s.astype(BF16)
        lhs_ref[rows, c:c + dk] = (qn * qd_f).astype(BF16)
        lhs_ref[rows, c + dk:c + 2 * dk] = (qn * qd_b).astype(BF16)
        rhs_ref[n, 0:c, :] = vn
        kvf_ref[n] = jnp.dot((ktn * kd_f).astype(BF16), vn, preferred_element_type=F32)
        kvb_ref[n] = jnp.dot((ktn * kd_b).astype(BF16), vn, preferred_element_type=F32)

    state = jnp.zeros((dk, RET_V_DIM), F32)
    for n in range(n_chunks):
        rhs_ref[n, c:c + dk, :] = state.astype(BF16)
        state = cd_f * state + kvf_ref[n]
    state = jnp.zeros((dk, RET_V_DIM), F32)
    for n in reversed(range(n_chunks)):
        rhs_ref[n, c + dk:c + 2 * dk, :] = state.astype(BF16)
        state = cd_b * state + kvb_ref[n]

    for n in range(n_chunks):
        rows = slice(n * c, (n + 1) * c)
        o = jnp.dot(lhs_ref[rows, :], rhs_ref[n], preferred_element_type=F32)
        o_ref[rows, :] = _layernorm_rows(o).astype(o_ref.dtype)


def _retention(proj, cos_r, sin_r, dec_f, dec_b):
    t = proj.shape[0]
    nb = t // SEQ
    tab = pl.BlockSpec((SEQ, RET_QK_DIM), lambda b, h: (0, 0))
    dec = pl.BlockSpec((None, 1, RET_V_DIM), lambda b, h: (h, 0, 0))
    return pl.pallas_call(
        _retention_kernel,
        grid=(nb, RET_HEADS),
        in_specs=[
            pl.BlockSpec((SEQ, RET_QK_DIM), lambda b, h: (b, OFF_RQ // RET_QK_DIM + h)),
            pl.BlockSpec((SEQ, RET_QK_DIM), lambda b, h: (b, OFF_RK // RET_QK_DIM + h)),
            pl.BlockSpec((SEQ, RET_V_DIM), lambda b, h: (b, OFF_RV // RET_V_DIM + h)),
            tab, tab, dec, dec,
        ],
        out_specs=pl.BlockSpec((SEQ, RET_V_DIM), lambda b, h: (b, h)),
        out_shape=jax.ShapeDtypeStruct((t, RET_V_W), BF16),
        scratch_shapes=[
            pltpu.VMEM((SEQ, RET_CHUNK + 2 * RET_QK_DIM), BF16),
            pltpu.VMEM((SEQ // RET_CHUNK, RET_CHUNK + 2 * RET_QK_DIM, RET_V_DIM), BF16),
            pltpu.VMEM((SEQ // RET_CHUNK, RET_QK_DIM, RET_V_DIM), F32),
            pltpu.VMEM((SEQ // RET_CHUNK, RET_QK_DIM, RET_V_DIM), F32),
        ],
        compiler_params=_cparams("parallel", "parallel"),
        name="retention",
    )(proj, proj, proj, cos_r, sin_r, dec_f, dec_b)


def _attention_kernel(q_ref, k_ref, vt_ref, o_ref, st_a, m_a, st_b, m_b):
    t = pl.program_id(0)

    @pl.when(t == 0)
    def _():
        st_b[...] = jnp.zeros(st_b.shape, F32)
        m_b[...] = jnp.zeros(m_b.shape, F32)

    def step(st_w, m_w, st_r, m_r):
        k = k_ref[...]
        vt = vt_ref[...]
        for j in range(ATT_Q_TILE // ATT_SUB_TILE):
            rows = slice(j * ATT_SUB_TILE, (j + 1) * ATT_SUB_TILE)
            st = lax.dot_general(k, q_ref[rows, :], (((1,), (1,)), ((), ())),
                                 preferred_element_type=F32)
            st_w[j] = st
            m_w[j] = jnp.max(st, axis=0, keepdims=True)
            p = jnp.exp2(st_r[j] - m_r[j])
            l = jnp.sum(p, axis=0, keepdims=True)
            ot = jnp.dot(vt, p.astype(BF16), preferred_element_type=F32)
            o_ref[rows, :] = (ot / l).T.astype(o_ref.dtype)

    parity = lax.rem(t, 2)
    pl.when(parity == 0)(lambda: step(st_a, m_a, st_b, m_b))
    pl.when(parity == 1)(lambda: step(st_b, m_b, st_a, m_a))


def _attention(qf, kf, vt):
    t = qf.shape[0]
    nb = t // SEQ
    q_tiles = SEQ // ATT_Q_TILE
    n_tiles = nb * MLA_HEADS * q_tiles
    n_sub = ATT_Q_TILE // ATT_SUB_TILE

    def tile_of(u):
        return u // (MLA_HEADS * q_tiles), (u // q_tiles) % MLA_HEADS, u % q_tiles

    def cur(s):
        return tile_of(jnp.minimum(s, n_tiles - 1))

    def prv(s):
        return tile_of(jnp.maximum(s - 1, 0))

    def q_map(s):
        b, h, i = cur(s)
        return (b * q_tiles + i, h)

    def k_map(s):
        b, h, _ = cur(s)
        return (b, h)

    def vt_map(s):
        b, h, _ = prv(s)
        return (b, h, 0)

    def o_map(s):
        b, h, i = prv(s)
        return (b * q_tiles + i, h)

    score_buf = pltpu.VMEM((n_sub, SEQ, ATT_SUB_TILE), F32)
    max_buf = pltpu.VMEM((n_sub, 1, ATT_SUB_TILE), F32)
    return pl.pallas_call(
        _attention_kernel,
        grid=(n_tiles + 1,),
        in_specs=[
            pl.BlockSpec((ATT_Q_TILE, MLA_HEAD_PAD), q_map),
            pl.BlockSpec((SEQ, MLA_HEAD_PAD), k_map),
            pl.BlockSpec((None, MLA_V, SEQ), vt_map),
        ],
        out_specs=pl.BlockSpec((ATT_Q_TILE, MLA_V), o_map),
        out_shape=jax.ShapeDtypeStruct((t, MLA_V_W), BF16),
        scratch_shapes=[score_buf, max_buf, score_buf, max_buf],
        compiler_params=_cparams("arbitrary"),
        name="mla_attention",
    )(qf, kf, vt)


def _merge_kernel(x_ref, ron_ref, rg_ref, att_ref, ga_ref, gb_ref, g1_ref, gn_ref,
                  wro_ref, wmo_ref, wout_ref, lng_ref, lnb_ref, o_ref):
    rg = rg_ref[...].astype(F32)
    ya_in = (rg * jax.nn.sigmoid(rg)) * (ron_ref[...].astype(F32) * gn_ref[...])
    y_a = jnp.dot(ya_in.astype(BF16), wro_ref[...], preferred_element_type=F32)
    y_b = jnp.dot(att_ref[...], wmo_ref[...], preferred_element_type=F32)
    merged = (jax.nn.sigmoid(ga_ref[...].astype(F32)) * y_a
              + jax.nn.sigmoid(gb_ref[...].astype(F32)) * y_b)
    f = jnp.dot(merged.astype(BF16), wout_ref[...], preferred_element_type=F32)
    z = DEEPNORM_ALPHA * x_ref[...] + (1.0 + g1_ref[...]) * f
    o_ref[...] = _layernorm_rows(z) * lng_ref[...] + lnb_ref[...]


def _merge(x, ron, proj, att, g1, gn_g, w_ro, w_mo, w_out, ln_g, ln_b):
    t = x.shape[0]
    tiles_per_seq = SEQ // ROW_TILE
    row = lambda i: (i, 0)
    full = lambda i: (0, 0)
    wide = pl.BlockSpec((ROW_TILE, D_MODEL), row)
    vec = pl.BlockSpec((1, D_MODEL), full)
    mat = pl.BlockSpec((D_MODEL, D_MODEL), full)
    return pl.pallas_call(
        _merge_kernel,
        grid=(t // ROW_TILE,),
        in_specs=[
            wide, wide,
            pl.BlockSpec((ROW_TILE, D_MODEL), lambda i: (i, OFF_RG // D_MODEL)),
            wide,
            pl.BlockSpec((ROW_TILE, D_MODEL), lambda i: (i, OFF_GA // D_MODEL)),
            pl.BlockSpec((ROW_TILE, D_MODEL), lambda i: (i, OFF_GB // D_MODEL)),
            pl.BlockSpec((None, 1, D_MODEL), lambda i: (i // tiles_per_seq, 0, 0)),
            vec, mat, mat, mat, vec, vec,
        ],
        out_specs=wide,
        out_shape=jax.ShapeDtypeStruct((t, D_MODEL), F32),
        compiler_params=_cparams("parallel"),
        name="merge_out",
    )(x, ron, proj, att, proj, proj, g1, gn_g, w_ro, w_mo, w_out, ln_g, ln_b)


def _ffn_in_kernel(x_ref, xp_ref, xn_ref, sc_ref, sh_ref, w_ref, cw_ref, cb_ref, o_ref, a_scr):
    tiles_per_seq = SEQ // ROW_TILE
    i = pl.program_id(0)
    tm = ROW_TILE
    first = (i % tiles_per_seq) == 0
    last = (i % tiles_per_seq) == tiles_per_seq - 1
    xe = jnp.concatenate([xp_ref[...], x_ref[...], xn_ref[...]], axis=0)
    he = (_layernorm_rows(xe) * (1.0 + sc_ref[...]) + sh_ref[...]).astype(BF16)
    h = he[8:8 + tm, :]
    inv_sqrt2 = 1.0 / math.sqrt(2.0)
    cw = cw_ref[...] * inv_sqrt2
    cb = cb_ref[...] * inv_sqrt2
    for c in range(D_FF // FF_COL_TILE):
        cols = slice(c * FF_COL_TILE, (c + 1) * FF_COL_TILE)
        a_e = jnp.dot(he, w_ref[:, cols], preferred_element_type=F32)
        a_scr[c, 0:8, :] = jnp.where(first, 0.0, a_e[0:8, :])
        a_scr[c, 8:8 + tm, :] = a_e[8:8 + tm, :]
        a_scr[c, 8 + tm:16 + tm, :] = jnp.where(last, 0.0, a_e[8 + tm:16 + tm, :])
        conv = (a_scr[c, 7:7 + tm, :] * cw[0:1, cols] + a_scr[c, 8:8 + tm, :] * cw[1:2, cols]
                + a_scr[c, 9:9 + tm, :] * cw[2:3, cols] + cb[:, cols])
        gate = jnp.dot(h, w_ref[:, D_FF + c * FF_COL_TILE:D_FF + (c + 1) * FF_COL_TILE],
                       preferred_element_type=F32)
        o_ref[:, cols] = (conv * (1.0 + lax.erf(conv)) * gate).astype(o_ref.dtype)


def _ffn_in(x, sc, sh, w_up, conv_w, conv_b):
    t = x.shape[0]
    tiles_per_seq = SEQ // ROW_TILE
    halo_per_tile = ROW_TILE // 8
    n_halo = t // 8
    full = lambda i: (0, 0)
    mod_spec = pl.BlockSpec((None, 1, D_MODEL), lambda i: (i // tiles_per_seq, 0, 0))
    return pl.pallas_call(
        _ffn_in_kernel,
        grid=(t // ROW_TILE,),
        in_specs=[
            pl.BlockSpec((ROW_TILE, D_MODEL), lambda i: (i, 0)),
            pl.BlockSpec((8, D_MODEL), lambda i: (jnp.maximum(i * halo_per_tile - 1, 0), 0)),
            pl.BlockSpec((8, D_MODEL), lambda i: (jnp.minimum((i + 1) * halo_per_tile, n_halo - 1), 0)),
            mod_spec, mod_spec,
            pl.BlockSpec((D_MODEL, 2 * D_FF), full),
            pl.BlockSpec((3, D_FF), full),
            pl.BlockSpec((1, D_FF), full),
        ],
        out_specs=pl.BlockSpec((ROW_TILE, D_FF), lambda i: (i, 0)),
        out_shape=jax.ShapeDtypeStruct((t, D_FF), BF16),
        scratch_shapes=[pltpu.VMEM((D_FF // FF_COL_TILE, ROW_TILE + 16, FF_COL_TILE), F32)],
        compiler_params=_cparams("parallel"),
        name="ffn_in",
    )(x, x, x, sc, sh, w_up, conv_w, conv_b)


def _ffn_out_kernel(x_ref, h_ref, g2_ref, wd_ref, lng_ref, lnb_ref, o_ref):
    y = jnp.dot(h_ref[...], wd_ref[...], preferred_element_type=F32)
    z = DEEPNORM_ALPHA * x_ref[...] + ((1.0 + g2_ref[...]) * (1.0 / math.sqrt(2.0))) * y
    o_ref[...] = _layernorm_rows(z) * lng_ref[...] + lnb_ref[...]


def _ffn_out(x, hmid, g2, w_down, ln_g, ln_b, first_tile, n_tiles):
    tiles_per_seq = SEQ // ROW_TILE
    row = lambda i: (i + first_tile, 0)
    full = lambda i: (0, 0)
    return pl.pallas_call(
        _ffn_out_kernel,
        grid=(n_tiles,),
        in_specs=[
            pl.BlockSpec((ROW_TILE, D_MODEL), row),
            pl.BlockSpec((ROW_TILE, D_FF), row),
            pl.BlockSpec((None, 1, D_MODEL), lambda i: ((i + first_tile) // tiles_per_seq, 0, 0)),
            pl.BlockSpec((D_FF, D_MODEL), full),
            pl.BlockSpec((1, D_MODEL), full),
            pl.BlockSpec((1, D_MODEL), full),
        ],
        out_specs=pl.BlockSpec((ROW_TILE, D_MODEL), lambda i: (i, 0)),
        out_shape=jax.ShapeDtypeStruct((n_tiles * ROW_TILE, D_MODEL), F32),
        compiler_params=_cparams("parallel"),
        name="ffn_out",
    )(x, hmid, g2, w_down, ln_g, ln_b)


def _rope_cos_sin(dim):
    inv = 1.0 / (ROPE_THETA ** (jnp.arange(0, dim, 2, dtype=F32) / dim))
    ang = jnp.arange(SEQ, dtype=F32)[:, None] * inv[None, :]
    return jnp.cos(ang), jnp.sin(ang)


def _swap_halves(w):
    half = w.shape[-1] // 2
    return jnp.concatenate([w[..., half:], w[..., :half]], axis=-1)


def _prep_w_in(w_in):
    rq, rk, rv, rg, dq, dkv, kr, ga, gb = jnp.split(
        w_in, [512, 1024, 2048, 3072, 3328, 3456, 3520, 4544], axis=-1)
    return jnp.concatenate([rq, rk, rv, rg, ga, gb, dq, dkv, kr, _swap_halves(kr)], axis=-1).astype(BF16)


def _prep_w_uq(w_uq):
    w = w_uq.reshape(DEPTH, Q_LORA, MLA_HEADS, MLA_QK)
    rope = w[..., MLA_NOPE:]
    w = jnp.concatenate([w[..., :MLA_NOPE], rope, _swap_halves(rope)], axis=-1)
    return w.reshape(DEPTH, Q_LORA, MLA_HEADS * MLA_HEAD_PAD).astype(BF16)


def kernel(x_prompt, x_sample, c_prompt, c_sample, w_ada, b_ada, w_in, ret_decay_fwd, ret_decay_bwd,
           ret_gn_g, w_ret_o, q_norm_g, kv_norm_g, w_uq, w_uk, w_uv, w_mla_o, w_out, ln1_g, ln1_b,
           w_up, conv_w, conv_b, w_down, ln2_g, ln2_b):
    nb_p = x_prompt.shape[0]
    x = jnp.concatenate([x_prompt, x_sample], axis=0)
    nb = x.shape[0]
    x = x.reshape(nb * SEQ, D_MODEL)
    n_tiles = nb * SEQ // ROW_TILE
    tiles_p = nb_p * SEQ // ROW_TILE
    c = jnp.concatenate([c_prompt, c_sample], axis=0)

    ada = _ada_table(c, w_ada, b_ada).reshape(DEPTH, 6, nb, 1, D_MODEL)

    cos_r, sin_r = _rope_cos_sin(RET_QK_DIM)
    cos_r = jnp.concatenate([cos_r, cos_r], axis=-1)
    sin_r = jnp.concatenate([-sin_r, sin_r], axis=-1)
    cos_m, sin_m = _rope_cos_sin(MLA_ROPE)
    cs_m = jnp.concatenate([cos_m, cos_m, -sin_m, sin_m], axis=-1)

    w_in_e = _prep_w_in(w_in)
    w_uq_e = _prep_w_uq(w_uq)
    w_ukv = jnp.concatenate([w_uk, w_uv], axis=-1).astype(BF16)
    w_ro, w_mo, w_o = w_ret_o.astype(BF16), w_mla_o.astype(BF16), w_out.astype(BF16)
    w_u, w_d = w_up.astype(BF16), w_down.astype(BF16)
    dec_f = jnp.broadcast_to(ret_decay_fwd[:, :, None, None], (DEPTH, RET_HEADS, 1, RET_V_DIM))
    dec_b = jnp.broadcast_to(ret_decay_bwd[:, :, None, None], (DEPTH, RET_HEADS, 1, RET_V_DIM))

    for l in range(DEPTH):
        sh1, sc1, g1, sh2, sc2, g2 = (ada[l, k] for k in range(6))
        proj = _ln_mod_mm(x, sc1, sh1, w_in_e[l], "in_proj")
        qf, kf, vt = _mla_proj(proj, cs_m, q_norm_g[l][None], kv_norm_g[l][None], w_uq_e[l], w_ukv[l])
        ron = _retention(proj, cos_r, sin_r, dec_f[l], dec_b[l])
        att = _attention(qf, kf, vt)
        x = _merge(x, ron, proj, att, g1, ret_gn_g[l][None], w_ro[l], w_mo[l], w_o[l],
                   ln1_g[l][None], ln1_b[l][None])
        hmid = _ffn_in(x, sc2, sh2, w_u[l], conv_w[l], conv_b[l][None])
        ffn_out = functools.partial(_ffn_out, x, hmid, g2, w_d[l], ln2_g[l][None], ln2_b[l][None])
        if l < DEPTH - 1:
            x = ffn_out(0, n_tiles)
        else:
            y_prompt = ffn_out(0, tiles_p)
            y_sample = ffn_out(tiles_p, n_tiles - tiles_p)

    return (y_prompt.reshape(nb_p, SEQ, D_MODEL), y_sample.reshape(nb - nb_p, SEQ, D_MODEL))
```

```python
import functools
import math

import jax
import jax.numpy as jnp
from jax import lax
from jax.experimental import pallas as pl
from jax.experimental.pallas import tpu as pltpu

D_MODEL = 1024
SEQ = 2048
DEPTH = 4
RET_HEADS = 4
RET_QK_DIM = 128
RET_V_DIM = 256
RET_QK_W = RET_HEADS * RET_QK_DIM
RET_V_W = RET_HEADS * RET_V_DIM
RET_CHUNK = 128
MLA_HEADS = 8
MLA_NOPE = 128
MLA_ROPE = 64
MLA_V = 128
Q_LORA = 256
KV_LORA = 128
MLA_QK = MLA_NOPE + MLA_ROPE
MLA_V_W = MLA_HEADS * MLA_V
MLA_HEAD_PAD = 256
D_FF = 2816
ROPE_THETA = 10000.0
LN_EPS = 1e-5
RMS_EPS = 1e-6
DEEPNORM_ALPHA = (2.0 * DEPTH) ** 0.25

IN_EXT_W = 5632
OFF_RQ, OFF_RK, OFF_RV, OFF_RG, OFF_GA, OFF_GB = 0, 512, 1024, 2048, 3072, 4096
OFF_DQ, OFF_DKV, OFF_KR = 5120, 5376, 5504

V7X_VMEM_LIMIT_BYTES = 56 * 1024 * 1024
ROW_TILE = 512
UP_ROW_TILE = 1024
MM_COL_TILE = 512
FFN_K_TILE = 256
ATT_Q_TILE = 2048
ATT_SUB_TILE = 512
LOG2_E = 1.4426950408889634

BF16 = jnp.bfloat16
F32 = jnp.float32


def _cparams(*sem):
    return pltpu.CompilerParams(dimension_semantics=sem, vmem_limit_bytes=V7X_VMEM_LIMIT_BYTES)


def _resident(block_shape):
    return pl.BlockSpec(block_shape, lambda *_: (0,) * len(block_shape), pipeline_mode=pl.Buffered(1))


def _layernorm_rows(x):
    mu = jnp.mean(x, axis=-1, keepdims=True)
    xc = x - mu
    var = jnp.mean(xc * xc, axis=-1, keepdims=True)
    return xc * lax.rsqrt(var + LN_EPS)


def _ada_kernel(c_ref, w_ref, b_ref, o_ref):
    c = c_ref[...]
    s = (c * jax.nn.sigmoid(c)).astype(BF16)
    o_ref[...] = jnp.dot(s, w_ref[...].astype(BF16), preferred_element_type=F32) + b_ref[...]


def _ada_table(c, w_ada, b_ada):
    nb = c.shape[0]
    return pl.pallas_call(
        _ada_kernel,
        grid=(DEPTH, 6),
        in_specs=[
            pl.BlockSpec((nb, D_MODEL), lambda l, j: (0, 0)),
            pl.BlockSpec((None, D_MODEL, D_MODEL), lambda l, j: (l, 0, j)),
            pl.BlockSpec((None, 1, D_MODEL), lambda l, j: (l, 0, j)),
        ],
        out_specs=pl.BlockSpec((None, None, nb, D_MODEL), lambda l, j: (l, j, 0, 0)),
        out_shape=jax.ShapeDtypeStruct((DEPTH, 6, nb, D_MODEL), F32),
        compiler_params=_cparams("parallel", "parallel"),
        name="ada_table",
    )(c, w_ada, b_ada.reshape(DEPTH, 1, 6 * D_MODEL))


def _up_proj_kernel(x_ref, sc_ref, sh_ref, w_ref, o_ref):
    h = _layernorm_rows(x_ref[...]) * (1.0 + sc_ref[...]) + sh_ref[...]
    hb = h.astype(BF16)
    n_out = o_ref.shape[1]
    for n in range(n_out // MM_COL_TILE):
        cols = slice(n * MM_COL_TILE, (n + 1) * MM_COL_TILE)
        o_ref[:, cols] = jnp.dot(hb, w_ref[:, cols], preferred_element_type=F32).astype(o_ref.dtype)


def _up_proj(x, sc, sh, w):
    t = x.shape[0]
    n_out = w.shape[1]
    tiles_per_seq = SEQ // UP_ROW_TILE
    mod_spec = pl.BlockSpec((None, 1, D_MODEL), lambda i: (i // tiles_per_seq, 0, 0))
    return pl.pallas_call(
        _up_proj_kernel,
        grid=(t // UP_ROW_TILE,),
        in_specs=[
            pl.BlockSpec((UP_ROW_TILE, D_MODEL), lambda i: (i, 0)),
            mod_spec, mod_spec,
            _resident((D_MODEL, n_out)),
        ],
        out_specs=pl.BlockSpec((UP_ROW_TILE, n_out), lambda i: (i, 0)),
        out_shape=jax.ShapeDtypeStruct((t, n_out), BF16),
        compiler_params=_cparams("parallel"),
        name="up_proj",
    )(x, sc, sh, w)


def _rms_rows(x, g):
    return x * lax.rsqrt(jnp.mean(x * x, axis=-1, keepdims=True) + RMS_EPS) * g


def _in_proj_kernel(x_ref, sc_ref, sh_ref, w_ref, cs_ref, qg_ref, kvg_ref, wq_ref, wkv_ref,
                    proj_ref, qf_ref, kf_ref, vt_ref):
    hb = (_layernorm_rows(x_ref[...]) * (1.0 + sc_ref[...]) + sh_ref[...]).astype(BF16)

    def wide_cols(n):
        cols = slice(n * MM_COL_TILE, (n + 1) * MM_COL_TILE)
        proj_ref[:, cols] = jnp.dot(hb, w_ref[:, cols], preferred_element_type=F32).astype(proj_ref.dtype)

    lat = jnp.dot(hb, w_ref[:, OFF_DQ:IN_EXT_W], preferred_element_type=F32)
    wide_cols(0)

    scale = MLA_QK ** -0.5 * LOG2_E
    cs = cs_ref[...]
    cq = _rms_rows(lat[:, 0:Q_LORA], qg_ref[...]).astype(BF16)
    ckv = _rms_rows(lat[:, Q_LORA:Q_LORA + KV_LORA], kvg_ref[...]).astype(BF16)
    tk = lat[:, Q_LORA + KV_LORA:] * cs
    lane = lax.broadcasted_iota(jnp.int32, tk.shape, 1)
    k_rope = jnp.where(lane < MLA_ROPE, tk + pltpu.roll(tk, MLA_ROPE, axis=1), 0.0).astype(BF16)
    kv = jnp.dot(ckv, wkv_ref[...], preferred_element_type=F32)
    vt_ref[...] = kv[:, MLA_V_W:].T.astype(BF16)
    for h in range(MLA_HEADS):
        base = h * MLA_HEAD_PAD
        qh = jnp.dot(cq, wq_ref[:, base:base + MLA_HEAD_PAD], preferred_element_type=F32)
        tq = qh[:, MLA_NOPE:] * cs
        q_rope = tq + pltpu.roll(tq, MLA_ROPE, axis=1)
        qf_ref[:, base:base + MLA_NOPE] = (qh[:, :MLA_NOPE] * scale).astype(BF16)
        qf_ref[:, base + MLA_NOPE:base + MLA_HEAD_PAD] = (q_rope * scale).astype(BF16)
        kf_ref[:, base:base + MLA_NOPE] = kv[:, h * MLA_NOPE:(h + 1) * MLA_NOPE].astype(BF16)
        kf_ref[:, base + MLA_NOPE:base + MLA_HEAD_PAD] = k_rope

    for n in range(1, OFF_DQ // MM_COL_TILE):
        wide_cols(n)


def _in_proj(x, sc, sh, w_in_e, cs_m, q_norm_g, kv_norm_g, wq_ext, wkv):
    t = x.shape[0]
    tiles_per_seq = SEQ // ROW_TILE
    wide = MLA_HEADS * MLA_HEAD_PAD
    row = lambda i: (i, 0)
    mod_spec = pl.BlockSpec((None, 1, D_MODEL), lambda i: (i // tiles_per_seq, 0, 0))
    return pl.pallas_call(
        _in_proj_kernel,
        grid=(t // ROW_TILE,),
        in_specs=[
            pl.BlockSpec((ROW_TILE, D_MODEL), row),
            mod_spec, mod_spec,
            _resident((D_MODEL, IN_EXT_W)),
            pl.BlockSpec((ROW_TILE, 128), lambda i: (i % tiles_per_seq, 0)),
            _resident((1, Q_LORA)),
            _resident((1, KV_LORA)),
            _resident((Q_LORA, wide)),
            _resident((KV_LORA, 2 * MLA_V_W)),
        ],
        out_specs=[
            pl.BlockSpec((ROW_TILE, OFF_DQ), row),
            pl.BlockSpec((ROW_TILE, wide), row),
            pl.BlockSpec((ROW_TILE, wide), row),
            pl.BlockSpec((None, MLA_V_W, ROW_TILE), lambda i: (i // tiles_per_seq, 0, i % tiles_per_seq)),
        ],
        out_shape=[
            jax.ShapeDtypeStruct((t, OFF_DQ), BF16),
            jax.ShapeDtypeStruct((t, wide), BF16),
            jax.ShapeDtypeStruct((t, wide), BF16),
            jax.ShapeDtypeStruct((t // SEQ, MLA_V_W, SEQ), BF16),
        ],
        compiler_params=_cparams("parallel"),
        name="in_proj",
    )(x, sc, sh, w_in_e, cs_m, q_norm_g, kv_norm_g, wq_ext, wkv)


def _retention_kernel(q_ref, k_ref, v_ref, cos_ref, sin_ref, df_ref, db_ref, o_ref,
                      lhs_ref, rhs_ref, kvf_ref, kvb_ref):
    c = RET_CHUNK
    n_chunks = SEQ // c
    dk = RET_QK_DIM
    qf = q_ref[...].astype(F32)
    kf = k_ref[...].astype(F32)
    cos = cos_ref[...]
    sin = sin_ref[...]
    q = qf * cos + pltpu.roll(qf, dk // 2, axis=1) * sin
    k = (kf * cos + pltpu.roll(kf, dk // 2, axis=1) * sin) * (dk ** -0.5)
    kt = k.T

    lg_f = jnp.log(jax.nn.sigmoid(df_ref[...]))
    lg_b = jnp.log(jax.nn.sigmoid(db_ref[...]))
    lgf = lg_f[:, :c]
    lgb = lg_b[:, :c]
    row = lax.broadcasted_iota(jnp.int32, (c, c), 0).astype(F32)
    col = lax.broadcasted_iota(jnp.int32, (c, c), 1).astype(F32)
    diff = row - col
    dmat = jnp.where(diff >= 0, jnp.exp(lgf * jnp.maximum(diff, 0.0)),
                     jnp.exp(lgb * jnp.maximum(-diff, 0.0)))
    qd_f = jnp.exp(lgf * (row + 1.0))
    qd_b = jnp.exp(lgb * (c - row))
    pos = col[0:1, :]
    kd_f = jnp.exp(lgf * (c - 1.0 - pos))
    kd_b = jnp.exp(lgb * pos)
    cd_f = jnp.exp(lg_f * float(c))
    cd_b = jnp.exp(lg_b * float(c))

    for n in range(n_chunks):
        rows = slice(n * c, (n + 1) * c)
        qn = q[rows, :]
        ktn = kt[:, rows]
        vn = v_ref[rows, :]
        s = jnp.dot(qn.astype(BF16), ktn.astype(BF16), preferred_element_type=F32) * dmat
        lhs_ref[rows, 0:c] = s.astype(BF16)
        lhs_ref[rows, c:c + dk] = (qn * qd_f).astype(BF16)
        lhs_ref[rows, c + dk:c + 2 * dk] = (qn * qd_b).astype(BF16)
        rhs_ref[n, 0:c, :] = vn
        kvf_ref[n] = jnp.dot((ktn * kd_f).astype(BF16), vn, preferred_element_type=F32)
        kvb_ref[n] = jnp.dot((ktn * kd_b).astype(BF16), vn, preferred_element_type=F32)

    state = jnp.zeros((dk, RET_V_DIM), F32)
    for n in range(n_chunks):
        rhs_ref[n, c:c + dk, :] = state.astype(BF16)
        state = cd_f * state + kvf_ref[n]
    state = jnp.zeros((dk, RET_V_DIM), F32)
    for n in reversed(range(n_chunks)):
        rhs_ref[n, c + dk:c + 2 * dk, :] = state.astype(BF16)
        state = cd_b * state + kvb_ref[n]

    for n in range(n_chunks):
        rows = slice(n * c, (n + 1) * c)
        o = jnp.dot(lhs_ref[rows, :], rhs_ref[n], preferred_element_type=F32)
        o_ref[rows, :] = _layernorm_rows(o).astype(o_ref.dtype)


def _retention(proj, cos_r, sin_r, dec_f, dec_b):
    t = proj.shape[0]
    nb = t // SEQ
    tab = pl.BlockSpec((SEQ, RET_QK_DIM), lambda b, h: (0, 0))
    dec = pl.BlockSpec((None, 1, RET_V_DIM), lambda b, h: (h, 0, 0))
    return pl.pallas_call(
        _retention_kernel,
        grid=(nb, RET_HEADS),
        in_specs=[
            pl.BlockSpec((SEQ, RET_QK_DIM), lambda b, h: (b, OFF_RQ // RET_QK_DIM + h)),
            pl.BlockSpec((SEQ, RET_QK_DIM), lambda b, h: (b, OFF_RK // RET_QK_DIM + h)),
            pl.BlockSpec((SEQ, RET_V_DIM), lambda b, h: (b, OFF_RV // RET_V_DIM + h)),
            tab, tab, dec, dec,
        ],
        out_specs=pl.BlockSpec((SEQ, RET_V_DIM), lambda b, h: (b, h)),
        out_shape=jax.ShapeDtypeStruct((t, RET_V_W), BF16),
        scratch_shapes=[
            pltpu.VMEM((SEQ, RET_CHUNK + 2 * RET_QK_DIM), BF16),
            pltpu.VMEM((SEQ // RET_CHUNK, RET_CHUNK + 2 * RET_QK_DIM, RET_V_DIM), BF16),
            pltpu.VMEM((SEQ // RET_CHUNK, RET_QK_DIM, RET_V_DIM), F32),
            pltpu.VMEM((SEQ // RET_CHUNK, RET_QK_DIM, RET_V_DIM), F32),
        ],
        compiler_params=_cparams("parallel", "parallel"),
        name="retention",
    )(proj, proj, proj, cos_r, sin_r, dec_f, dec_b)


def _attention_kernel(q_ref, k_ref, vt_ref, o_ref, st_a, m_a, st_b, m_b):
    t = pl.program_id(0)

    @pl.when(t == 0)
    def _():
        st_b[...] = jnp.zeros(st_b.shape, F32)
        m_b[...] = jnp.zeros(m_b.shape, F32)

    def step(st_w, m_w, st_r, m_r):
        k = k_ref[...]
        vt = vt_ref[...]
        for j in range(ATT_Q_TILE // ATT_SUB_TILE):
            rows = slice(j * ATT_SUB_TILE, (j + 1) * ATT_SUB_TILE)
            st = lax.dot_general(k, q_ref[rows, :], (((1,), (1,)), ((), ())),
                                 preferred_element_type=F32)
            st_w[j] = st
            m_w[j] = jnp.max(st, axis=0, keepdims=True)
            p = jnp.exp2(st_r[j] - m_r[j])
            l = jnp.sum(p, axis=0, keepdims=True)
            ot = jnp.dot(vt, p.astype(BF16), preferred_element_type=F32)
            o_ref[rows, :] = (ot / l).T.astype(o_ref.dtype)

    parity = lax.rem(t, 2)
    pl.when(parity == 0)(lambda: step(st_a, m_a, st_b, m_b))
    pl.when(parity == 1)(lambda: step(st_b, m_b, st_a, m_a))


def _attention(qf, kf, vt):
    t = qf.shape[0]
    nb = t // SEQ
    q_tiles = SEQ // ATT_Q_TILE
    n_tiles = nb * MLA_HEADS * q_tiles
    n_sub = ATT_Q_TILE // ATT_SUB_TILE

    def tile_of(u):
        return u // (MLA_HEADS * q_tiles), (u // q_tiles) % MLA_HEADS, u % q_tiles

    def cur(s):
        return tile_of(jnp.minimum(s, n_tiles - 1))

    def prv(s):
        return tile_of(jnp.maximum(s - 1, 0))

    def q_map(s):
        b, h, i = cur(s)
        return (b * q_tiles + i, h)

    def k_map(s):
        b, h, _ = cur(s)
        return (b, h)

    def vt_map(s):
        b, h, _ = prv(s)
        return (b, h, 0)

    def o_map(s):
        b, h, i = prv(s)
        return (b * q_tiles + i, h)

    score_buf = pltpu.VMEM((n_sub, SEQ, ATT_SUB_TILE), F32)
    max_buf = pltpu.VMEM((n_sub, 1, ATT_SUB_TILE), F32)
    return pl.pallas_call(
        _attention_kernel,
        grid=(n_tiles + 1,),
        in_specs=[
            pl.BlockSpec((ATT_Q_TILE, MLA_HEAD_PAD), q_map),
            pl.BlockSpec((SEQ, MLA_HEAD_PAD), k_map),
            pl.BlockSpec((None, MLA_V, SEQ), vt_map),
        ],
        out_specs=pl.BlockSpec((ATT_Q_TILE, MLA_V), o_map),
        out_shape=jax.ShapeDtypeStruct((t, MLA_V_W), BF16),
        scratch_shapes=[score_buf, max_buf, score_buf, max_buf],
        compiler_params=_cparams("arbitrary"),
        name="mla_attention",
    )(qf, kf, vt)


def _merge_kernel(x_ref, ron_ref, rg_ref, att_ref, ga_ref, gb_ref, g1_ref, gn_ref,
                  wro_ref, wmo_ref, wout_ref, lng_ref, lnb_ref, o_ref):
    rg = rg_ref[...].astype(F32)
    ya_in = (rg * jax.nn.sigmoid(rg)) * (ron_ref[...].astype(F32) * gn_ref[...])
    y_a = jnp.dot(ya_in.astype(BF16), wro_ref[...], preferred_element_type=F32)
    y_b = jnp.dot(att_ref[...], wmo_ref[...], preferred_element_type=F32)
    merged = (jax.nn.sigmoid(ga_ref[...].astype(F32)) * y_a
              + jax.nn.sigmoid(gb_ref[...].astype(F32)) * y_b)
    f = jnp.dot(merged.astype(BF16), wout_ref[...], preferred_element_type=F32)
    z = DEEPNORM_ALPHA * x_ref[...] + (1.0 + g1_ref[...]) * f
    o_ref[...] = _layernorm_rows(z) * lng_ref[...] + lnb_ref[...]


def _merge(x, ron, proj, att, g1, gn_g, w_ro, w_mo, w_out, ln_g, ln_b):
    t = x.shape[0]
    tiles_per_seq = SEQ // ROW_TILE
    row = lambda i: (i, 0)
    wide = pl.BlockSpec((ROW_TILE, D_MODEL), row)
    vec = _resident((1, D_MODEL))
    mat = _resident((D_MODEL, D_MODEL))
    return pl.pallas_call(
        _merge_kernel,
        grid=(t // ROW_TILE,),
        in_specs=[
            wide, wide,
            pl.BlockSpec((ROW_TILE, D_MODEL), lambda i: (i, OFF_RG // D_MODEL)),
            wide,
            pl.BlockSpec((ROW_TILE, D_MODEL), lambda i: (i, OFF_GA // D_MODEL)),
            pl.BlockSpec((ROW_TILE, D_MODEL), lambda i: (i, OFF_GB // D_MODEL)),
            pl.BlockSpec((None, 1, D_MODEL), lambda i: (i // tiles_per_seq, 0, 0)),
            vec, mat, mat, mat, vec, vec,
        ],
        out_specs=wide,
        out_shape=jax.ShapeDtypeStruct((t, D_MODEL), F32),
        compiler_params=_cparams("parallel"),
        name="merge_out",
    )(x, ron, proj, att, proj, proj, g1, gn_g, w_ro, w_mo, w_out, ln_g, ln_b)


def _ffn_out_kernel(x_ref, a_ref, b_ref, ap_ref, an_ref, g2_ref, cw_ref, cb_ref, wd_ref,
                    lng_ref, lnb_ref, o_ref, *, first_tile):
    tiles_per_seq = SEQ // ROW_TILE
    i = pl.program_id(0) + first_tile
    tm = ROW_TILE
    first = (i % tiles_per_seq) == 0
    last = (i % tiles_per_seq) == tiles_per_seq - 1
    inv_sqrt2 = 1.0 / math.sqrt(2.0)
    cw = cw_ref[...] * inv_sqrt2
    cb = cb_ref[...] * inv_sqrt2
    prev_row = jnp.where(first, 0.0, ap_ref[7:8, :].astype(F32))
    next_row = jnp.where(last, 0.0, an_ref[0:1, :].astype(F32))
    r = lax.broadcasted_iota(jnp.int32, (tm, FFN_K_TILE), 0)
    y = None
    for c in range(D_FF // FFN_K_TILE):
        cols = slice(c * FFN_K_TILE, (c + 1) * FFN_K_TILE)
        a = a_ref[:, cols].astype(F32)
        a_up = jnp.where(r == 0, prev_row[:, cols], pltpu.roll(a, 1, axis=0))
        a_dn = jnp.where(r == tm - 1, next_row[:, cols], pltpu.roll(a, tm - 1, axis=0))
        conv = a_up * cw[0:1, cols] + a * cw[1:2, cols] + a_dn * cw[2:3, cols] + cb[:, cols]
        hmid = (conv * (1.0 + lax.erf(conv))).astype(BF16) * b_ref[:, cols]
        part = jnp.dot(hmid, wd_ref[cols, :], preferred_element_type=F32)
        y = part if y is None else y + part
    z = DEEPNORM_ALPHA * x_ref[...] + ((1.0 + g2_ref[...]) * inv_sqrt2) * y
    o_ref[...] = _layernorm_rows(z) * lng_ref[...] + lnb_ref[...]


def _ffn_out(x, u, g2, conv_w, conv_b, w_down, ln_g, ln_b, first_tile, n_tiles):
    tiles_per_seq = SEQ // ROW_TILE
    halo_per_tile = ROW_TILE // 8
    n_halo = x.shape[0] // 8
    row = lambda i: (i + first_tile, 0)
    return pl.pallas_call(
        functools.partial(_ffn_out_kernel, first_tile=first_tile),
        grid=(n_tiles,),
        in_specs=[
            pl.BlockSpec((ROW_TILE, D_MODEL), row),
            pl.BlockSpec((ROW_TILE, D_FF), lambda i: (i + first_tile, 0)),
            pl.BlockSpec((ROW_TILE, D_FF), lambda i: (i + first_tile, 1)),
            pl.BlockSpec((8, D_FF), lambda i: (jnp.maximum((i + first_tile) * halo_per_tile - 1, 0), 0)),
            pl.BlockSpec((8, D_FF),
                         lambda i: (jnp.minimum((i + first_tile + 1) * halo_per_tile, n_halo - 1), 0)),
            pl.BlockSpec((None, 1, D_MODEL), lambda i: ((i + first_tile) // tiles_per_seq, 0, 0)),
            _resident((3, D_FF)),
            _resident((1, D_FF)),
            _resident((D_FF, D_MODEL)),
            _resident((1, D_MODEL)),
            _resident((1, D_MODEL)),
        ],
        out_specs=pl.BlockSpec((ROW_TILE, D_MODEL), lambda i: (i, 0)),
        out_shape=jax.ShapeDtypeStruct((n_tiles * ROW_TILE, D_MODEL), F32),
        compiler_params=_cparams("parallel"),
        name="ffn_out",
    )(x, u, u, u, u, g2, conv_w, conv_b, w_down, ln_g, ln_b)


def _rope_cos_sin(dim):
    inv = 1.0 / (ROPE_THETA ** (jnp.arange(0, dim, 2, dtype=F32) / dim))
    ang = jnp.arange(SEQ, dtype=F32)[:, None] * inv[None, :]
    return jnp.cos(ang), jnp.sin(ang)


def _swap_halves(w):
    half = w.shape[-1] // 2
    return jnp.concatenate([w[..., half:], w[..., :half]], axis=-1)


def _prep_w_in(w_in):
    rq, rk, rv, rg, dq, dkv, kr, ga, gb = jnp.split(
        w_in, [512, 1024, 2048, 3072, 3328, 3456, 3520, 4544], axis=-1)
    return jnp.concatenate([rq, rk, rv, rg, ga, gb, dq, dkv, kr, _swap_halves(kr)], axis=-1).astype(BF16)


def _prep_w_uq(w_uq):
    w = w_uq.reshape(DEPTH, Q_LORA, MLA_HEADS, MLA_QK)
    rope = w[..., MLA_NOPE:]
    w = jnp.concatenate([w[..., :MLA_NOPE], rope, _swap_halves(rope)], axis=-1)
    return w.reshape(DEPTH, Q_LORA, MLA_HEADS * MLA_HEAD_PAD).astype(BF16)


def kernel(x_prompt, x_sample, c_prompt, c_sample, w_ada, b_ada, w_in, ret_decay_fwd, ret_decay_bwd,
           ret_gn_g, w_ret_o, q_norm_g, kv_norm_g, w_uq, w_uk, w_uv, w_mla_o, w_out, ln1_g, ln1_b,
           w_up, conv_w, conv_b, w_down, ln2_g, ln2_b):
    nb_p = x_prompt.shape[0]
    x = jnp.concatenate([x_prompt, x_sample], axis=0)
    nb = x.shape[0]
    x = x.reshape(nb * SEQ, D_MODEL)
    n_tiles = nb * SEQ // ROW_TILE
    tiles_p = nb_p * SEQ // ROW_TILE
    c = jnp.concatenate([c_prompt, c_sample], axis=0)

    ada = _ada_table(c, w_ada, b_ada).reshape(DEPTH, 6, nb, 1, D_MODEL)

    cos_r, sin_r = _rope_cos_sin(RET_QK_DIM)
    cos_r = jnp.concatenate([cos_r, cos_r], axis=-1)
    sin_r = jnp.concatenate([-sin_r, sin_r], axis=-1)
    cos_m, sin_m = _rope_cos_sin(MLA_ROPE)
    cs_m = jnp.concatenate([cos_m, cos_m, -sin_m, sin_m], axis=-1)

    w_in_e = _prep_w_in(w_in)
    w_uq_e = _prep_w_uq(w_uq)
    w_ukv = jnp.concatenate([w_uk, w_uv], axis=-1).astype(BF16)
    w_ro, w_mo, w_o = w_ret_o.astype(BF16), w_mla_o.astype(BF16), w_out.astype(BF16)
    w_u, w_d = w_up.astype(BF16), w_down.astype(BF16)
    dec_f = jnp.broadcast_to(ret_decay_fwd[:, :, None, None], (DEPTH, RET_HEADS, 1, RET_V_DIM))
    dec_b = jnp.broadcast_to(ret_decay_bwd[:, :, None, None], (DEPTH, RET_HEADS, 1, RET_V_DIM))

    for l in range(DEPTH):
        sh1, sc1, g1, sh2, sc2, g2 = (ada[l, k] for k in range(6))
        proj, qf, kf, vt = _in_proj(x, sc1, sh1, w_in_e[l], cs_m, q_norm_g[l][None], kv_norm_g[l][None],
                                    w_uq_e[l], w_ukv[l])
        ron = _retention(proj, cos_r, sin_r, dec_f[l], dec_b[l])
        att = _attention(qf, kf, vt)
        x = _merge(x, ron, proj, att, g1, ret_gn_g[l][None], w_ro[l], w_mo[l], w_o[l],
                   ln1_g[l][None], ln1_b[l][None])
        u = _up_proj(x, sc2, sh2, w_u[l])
        ffn_out = functools.partial(_ffn_out, x, u, g2, conv_w[l], conv_b[l][None], w_d[l],
                                    ln2_g[l][None], ln2_b[l][None])
        if l < DEPTH - 1:
            x = ffn_out(0, n_tiles)
        else:
            y_prompt = ffn_out(0, tiles_p)
            y_sample = ffn_out(tiles_p, n_tiles - tiles_p)

    return (y_prompt.reshape(nb_p, SEQ, D_MODEL), y_sample.reshape(nb - nb_p, SEQ, D_MODEL))
```

```python
import functools
import math

import jax
import jax.numpy as jnp
from jax import lax
from jax.experimental import pallas as pl
from jax.experimental.pallas import tpu as pltpu

D_MODEL = 1024
SEQ = 2048
DEPTH = 4
RET_HEADS = 4
RET_QK_DIM = 128
RET_V_DIM = 256
RET_QK_W = RET_HEADS * RET_QK_DIM
RET_V_W = RET_HEADS * RET_V_DIM
RET_CHUNK = 128
MLA_HEADS = 8
MLA_NOPE = 128
MLA_ROPE = 64
MLA_V = 128
Q_LORA = 256
KV_LORA = 128
MLA_QK = MLA_NOPE + MLA_ROPE
MLA_V_W = MLA_HEADS * MLA_V
MLA_HEAD_PAD = 256
D_FF = 2816
ROPE_THETA = 10000.0
LN_EPS = 1e-5
RMS_EPS = 1e-6
DEEPNORM_ALPHA = (2.0 * DEPTH) ** 0.25

IN_EXT_W = 5632
OFF_RQ, OFF_RK, OFF_RV, OFF_RG, OFF_GA, OFF_GB = 0, 512, 1024, 2048, 3072, 4096
OFF_DQ, OFF_DKV, OFF_KR = 5120, 5376, 5504

V7X_VMEM_LIMIT_BYTES = 56 * 1024 * 1024
ROW_TILE = 512
UP_ROW_TILE = 1024
MM_COL_TILE = 512
FFN_K_TILE = 256
ATT_Q_TILE = 2048
ATT_SUB_TILE = 512
LOG2_E = 1.4426950408889634

BF16 = jnp.bfloat16
F32 = jnp.float32


def _cparams(*sem):
    return pltpu.CompilerParams(dimension_semantics=sem, vmem_limit_bytes=V7X_VMEM_LIMIT_BYTES)


def _resident(block_shape):
    return pl.BlockSpec(block_shape, lambda *_: (0,) * len(block_shape), pipeline_mode=pl.Buffered(1))


def _layernorm_rows(x):
    mu = jnp.mean(x, axis=-1, keepdims=True)
    xc = x - mu
    var = jnp.mean(xc * xc, axis=-1, keepdims=True)
    return xc * lax.rsqrt(var + LN_EPS)


def _ada_kernel(c_ref, w_ref, b_ref, o_ref):
    c = c_ref[...]
    s = (c * jax.nn.sigmoid(c)).astype(BF16)
    o_ref[...] = jnp.dot(s, w_ref[...].astype(BF16), preferred_element_type=F32) + b_ref[...]


def _ada_table(c, w_ada, b_ada):
    nb = c.shape[0]
    return pl.pallas_call(
        _ada_kernel,
        grid=(DEPTH, 6),
        in_specs=[
            pl.BlockSpec((nb, D_MODEL), lambda l, j: (0, 0)),
            pl.BlockSpec((None, D_MODEL, D_MODEL), lambda l, j: (l, 0, j)),
            pl.BlockSpec((None, 1, D_MODEL), lambda l, j: (l, 0, j)),
        ],
        out_specs=pl.BlockSpec((None, None, nb, D_MODEL), lambda l, j: (l, j, 0, 0)),
        out_shape=jax.ShapeDtypeStruct((DEPTH, 6, nb, D_MODEL), F32),
        compiler_params=_cparams("parallel", "parallel"),
        name="ada_table",
    )(c, w_ada, b_ada.reshape(DEPTH, 1, 6 * D_MODEL))


def _col_chunks(width):
    return [(c, min(MM_COL_TILE, width - c)) for c in range(0, width, MM_COL_TILE)]


def _up_proj_kernel(x_ref, sc_ref, sh_ref, w_ref, a_ref, b_ref, h_a, h_b):
    i = pl.program_id(0)

    @pl.when(i == 0)
    def _():
        h_b[...] = jnp.zeros(h_b.shape, BF16)

    def step(h_prev, h_cur):
        for c, wd in _col_chunks(D_FF):
            b_ref[:, c:c + wd] = jnp.dot(h_prev[...], w_ref[:, D_FF + c:D_FF + c + wd],
                                         preferred_element_type=F32).astype(b_ref.dtype)
        hb = (_layernorm_rows(x_ref[...]) * (1.0 + sc_ref[...]) + sh_ref[...]).astype(BF16)
        h_cur[...] = hb
        for c, wd in _col_chunks(D_FF):
            a_ref[:, c:c + wd] = jnp.dot(hb, w_ref[:, c:c + wd], preferred_element_type=F32).astype(a_ref.dtype)

    parity = lax.rem(i, 2)
    pl.when(parity == 0)(lambda: step(h_b, h_a))
    pl.when(parity == 1)(lambda: step(h_a, h_b))


def _up_proj(x, sc, sh, w):
    t = x.shape[0]
    n_tiles = t // UP_ROW_TILE
    tiles_per_seq = SEQ // UP_ROW_TILE
    cur = lambda i: jnp.minimum(i, n_tiles - 1)
    prv = lambda i: jnp.maximum(i - 1, 0)
    mod_spec = pl.BlockSpec((None, 1, D_MODEL), lambda i: (cur(i) // tiles_per_seq, 0, 0))
    h_buf = pltpu.VMEM((UP_ROW_TILE, D_MODEL), BF16)
    return pl.pallas_call(
        _up_proj_kernel,
        grid=(n_tiles + 1,),
        in_specs=[
            pl.BlockSpec((UP_ROW_TILE, D_MODEL), lambda i: (cur(i), 0)),
            mod_spec, mod_spec,
            _resident((D_MODEL, 2 * D_FF)),
        ],
        out_specs=[
            pl.BlockSpec((UP_ROW_TILE, D_FF), lambda i: (cur(i), 0)),
            pl.BlockSpec((UP_ROW_TILE, D_FF), lambda i: (prv(i), 0)),
        ],
        out_shape=[jax.ShapeDtypeStruct((t, D_FF), BF16), jax.ShapeDtypeStruct((t, D_FF), BF16)],
        scratch_shapes=[h_buf, h_buf],
        compiler_params=_cparams("arbitrary"),
        name="up_proj",
    )(x, sc, sh, w)


def _rms_rows(x, g):
    return x * lax.rsqrt(jnp.mean(x * x, axis=-1, keepdims=True) + RMS_EPS) * g


def _in_proj_kernel(x_ref, sc_ref, sh_ref, w_ref, cs_ref, cos_ref, sin_ref, qg_ref, kvg_ref, wq_ref, wkv_ref,
                    lo_ref, hi_ref, qf_ref, kf_ref, vt_ref, h_a, h_b):
    i = pl.program_id(0)

    @pl.when(i == 0)
    def _():
        h_b[...] = jnp.zeros(h_b.shape, BF16)

    def step(h_prev, h_cur):
        for c, wd in _col_chunks(OFF_DQ - OFF_RG):
            hi_ref[:, c:c + wd] = jnp.dot(h_prev[...], w_ref[:, OFF_RG + c:OFF_RG + c + wd],
                                          preferred_element_type=F32).astype(hi_ref.dtype)
        hb = (_layernorm_rows(x_ref[...]) * (1.0 + sc_ref[...]) + sh_ref[...]).astype(BF16)
        h_cur[...] = hb
        lat = jnp.dot(hb, w_ref[:, OFF_DQ:IN_EXT_W], preferred_element_type=F32)

        cos = cos_ref[...]
        sin = sin_ref[...]
        for c, wd in _col_chunks(OFF_RG):
            res = jnp.dot(hb, w_ref[:, c:c + wd], preferred_element_type=F32)
            if c < OFF_RV:
                gain = 1.0 if c < OFF_RK else RET_QK_DIM ** -0.5
                for h0 in range(0, wd, RET_QK_DIM):
                    xh = res[:, h0:h0 + RET_QK_DIM]
                    roped = (xh * cos + pltpu.roll(xh, RET_QK_DIM // 2, axis=1) * sin) * gain
                    lo_ref[:, c + h0:c + h0 + RET_QK_DIM] = roped.astype(lo_ref.dtype)
            else:
                lo_ref[:, c:c + wd] = res.astype(lo_ref.dtype)

        scale = MLA_QK ** -0.5 * LOG2_E
        cs = cs_ref[...]
        cq = _rms_rows(lat[:, 0:Q_LORA], qg_ref[...]).astype(BF16)
        ckv = _rms_rows(lat[:, Q_LORA:Q_LORA + KV_LORA], kvg_ref[...]).astype(BF16)
        tk = lat[:, Q_LORA + KV_LORA:] * cs
        lane = lax.broadcasted_iota(jnp.int32, tk.shape, 1)
        k_rope = jnp.where(lane < MLA_ROPE, tk + pltpu.roll(tk, MLA_ROPE, axis=1), 0.0).astype(BF16)
        kv = jnp.dot(ckv, wkv_ref[...], preferred_element_type=F32)
        vt_ref[...] = kv[:, MLA_V_W:].T.astype(BF16)
        for h in range(MLA_HEADS):
            base = h * MLA_HEAD_PAD
            qh = jnp.dot(cq, wq_ref[:, base:base + MLA_HEAD_PAD], preferred_element_type=F32)
            tq = qh[:, MLA_NOPE:] * cs
            q_rope = tq + pltpu.roll(tq, MLA_ROPE, axis=1)
            qf_ref[:, base:base + MLA_NOPE] = (qh[:, :MLA_NOPE] * scale).astype(BF16)
            qf_ref[:, base + MLA_NOPE:base + MLA_HEAD_PAD] = (q_rope * scale).astype(BF16)
            kf_ref[:, base:base + MLA_NOPE] = kv[:, h * MLA_NOPE:(h + 1) * MLA_NOPE].astype(BF16)
            kf_ref[:, base + MLA_NOPE:base + MLA_HEAD_PAD] = k_rope

    parity = lax.rem(i, 2)
    pl.when(parity == 0)(lambda: step(h_b, h_a))
    pl.when(parity == 1)(lambda: step(h_a, h_b))


def _in_proj(x, sc, sh, w_in_e, cs_m, cos_r, sin_r, q_norm_g, kv_norm_g, wq_ext, wkv):
    t = x.shape[0]
    n_tiles = t // ROW_TILE
    tiles_per_seq = SEQ // ROW_TILE
    wide = MLA_HEADS * MLA_HEAD_PAD
    cur = lambda i: jnp.minimum(i, n_tiles - 1)
    prv = lambda i: jnp.maximum(i - 1, 0)
    row = lambda i: (cur(i), 0)
    pos = lambda i: (cur(i) % tiles_per_seq, 0)
    mod_spec = pl.BlockSpec((None, 1, D_MODEL), lambda i: (cur(i) // tiles_per_seq, 0, 0))
    h_buf = pltpu.VMEM((ROW_TILE, D_MODEL), BF16)
    return pl.pallas_call(
        _in_proj_kernel,
        grid=(n_tiles + 1,),
        in_specs=[
            pl.BlockSpec((ROW_TILE, D_MODEL), row),
            mod_spec, mod_spec,
            _resident((D_MODEL, IN_EXT_W)),
            pl.BlockSpec((ROW_TILE, 128), pos),
            pl.BlockSpec((ROW_TILE, RET_QK_DIM), pos),
            pl.BlockSpec((ROW_TILE, RET_QK_DIM), pos),
            _resident((1, Q_LORA)),
            _resident((1, KV_LORA)),
            _resident((Q_LORA, wide)),
            _resident((KV_LORA, 2 * MLA_V_W)),
        ],
        out_specs=[
            pl.BlockSpec((ROW_TILE, OFF_RG), row),
            pl.BlockSpec((ROW_TILE, OFF_DQ - OFF_RG), lambda i: (prv(i), 0)),
            pl.BlockSpec((ROW_TILE, wide), row),
            pl.BlockSpec((ROW_TILE, wide), row),
            pl.BlockSpec((None, MLA_V_W, ROW_TILE), lambda i: (cur(i) // tiles_per_seq, 0, cur(i) % tiles_per_seq)),
        ],
        out_shape=[
            jax.ShapeDtypeStruct((t, OFF_RG), BF16),
            jax.ShapeDtypeStruct((t, OFF_DQ - OFF_RG), BF16),
            jax.ShapeDtypeStruct((t, wide), BF16),
            jax.ShapeDtypeStruct((t, wide), BF16),
            jax.ShapeDtypeStruct((t // SEQ, MLA_V_W, SEQ), BF16),
        ],
        scratch_shapes=[h_buf, h_buf],
        compiler_params=_cparams("arbitrary"),
        name="in_proj",
    )(x, sc, sh, w_in_e, cs_m, cos_r, sin_r, q_norm_g, kv_norm_g, wq_ext, wkv)


def _retention_kernel(q_ref, k_ref, v_ref, df_ref, db_ref, o_ref, lhs_ref, rhs_ref, kvf_ref, kvb_ref):
    c = RET_CHUNK
    n_chunks = SEQ // c
    dk = RET_QK_DIM
    q = q_ref[...].astype(F32)
    kt = k_ref[...].astype(F32).T

    lg_f = jnp.log(jax.nn.sigmoid(df_ref[...]))
    lg_b = jnp.log(jax.nn.sigmoid(db_ref[...]))
    lgf = lg_f[:, :c]
    lgb = lg_b[:, :c]
    row = lax.broadcasted_iota(jnp.int32, (c, c), 0).astype(F32)
    col = lax.broadcasted_iota(jnp.int32, (c, c), 1).astype(F32)
    diff = row - col
    dmat = jnp.where(diff >= 0, jnp.exp(lgf * jnp.maximum(diff, 0.0)),
                     jnp.exp(lgb * jnp.maximum(-diff, 0.0)))
    qd_f = jnp.exp(lgf * (row + 1.0))
    qd_b = jnp.exp(lgb * (c - row))
    pos = col[0:1, :]
    kd_f = jnp.exp(lgf * (c - 1.0 - pos))
    kd_b = jnp.exp(lgb * pos)
    cd_f = jnp.exp(lg_f * float(c))
    cd_b = jnp.exp(lg_b * float(c))

    for n in range(n_chunks):
        rows = slice(n * c, (n + 1) * c)
        qn = q[rows, :]
        ktn = kt[:, rows]
        vn = v_ref[rows, :]
        s = jnp.dot(qn.astype(BF16), ktn.astype(BF16), preferred_element_type=F32) * dmat
        lhs_ref[rows, 0:c] = s.astype(BF16)
        lhs_ref[rows, c:c + dk] = (qn * qd_f).astype(BF16)
        lhs_ref[rows, c + dk:c + 2 * dk] = (qn * qd_b).astype(BF16)
        rhs_ref[n, 0:c, :] = vn
        kvf_ref[n] = jnp.dot((ktn * kd_f).astype(BF16), vn, preferred_element_type=F32)
        kvb_ref[n] = jnp.dot((ktn * kd_b).astype(BF16), vn, preferred_element_type=F32)

    state = jnp.zeros((dk, RET_V_DIM), F32)
    for n in range(n_chunks):
        rhs_ref[n, c:c + dk, :] = state.astype(BF16)
        state = cd_f * state + kvf_ref[n]
    state = jnp.zeros((dk, RET_V_DIM), F32)
    for n in reversed(range(n_chunks)):
        rhs_ref[n, c + dk:c + 2 * dk, :] = state.astype(BF16)
        state = cd_b * state + kvb_ref[n]

    for n in range(n_chunks):
        rows = slice(n * c, (n + 1) * c)
        o = jnp.dot(lhs_ref[rows, :], rhs_ref[n], preferred_element_type=F32)
        o_ref[rows, :] = _layernorm_rows(o).astype(o_ref.dtype)


def _retention(proj, dec_f, dec_b):
    t = proj.shape[0]
    nb = t // SEQ
    dec = pl.BlockSpec((None, 1, RET_V_DIM), lambda b, h: (h, 0, 0))
    return pl.pallas_call(
        _retention_kernel,
        grid=(nb, RET_HEADS),
        in_specs=[
            pl.BlockSpec((SEQ, RET_QK_DIM), lambda b, h: (b, OFF_RQ // RET_QK_DIM + h)),
            pl.BlockSpec((SEQ, RET_QK_DIM), lambda b, h: (b, OFF_RK // RET_QK_DIM + h)),
            pl.BlockSpec((SEQ, RET_V_DIM), lambda b, h: (b, OFF_RV // RET_V_DIM + h)),
            dec, dec,
        ],
        out_specs=pl.BlockSpec((SEQ, RET_V_DIM), lambda b, h: (b, h)),
        out_shape=jax.ShapeDtypeStruct((t, RET_V_W), BF16),
        scratch_shapes=[
            pltpu.VMEM((SEQ, RET_CHUNK + 2 * RET_QK_DIM), BF16),
            pltpu.VMEM((SEQ // RET_CHUNK, RET_CHUNK + 2 * RET_QK_DIM, RET_V_DIM), BF16),
            pltpu.VMEM((SEQ // RET_CHUNK, RET_QK_DIM, RET_V_DIM), F32),
            pltpu.VMEM((SEQ // RET_CHUNK, RET_QK_DIM, RET_V_DIM), F32),
        ],
        compiler_params=_cparams("parallel", "parallel"),
        name="retention",
    )(proj, proj, proj, dec_f, dec_b)


def _attention_kernel(q_ref, k_ref, vt_ref, o_ref, st_a, m_a, st_b, m_b):
    t = pl.program_id(0)

    @pl.when(t == 0)
    def _():
        st_b[...] = jnp.zeros(st_b.shape, F32)
        m_b[...] = jnp.zeros(m_b.shape, F32)

    def step(st_w, m_w, st_r, m_r):
        k = k_ref[...]
        vt = jnp.concatenate([vt_ref[...], jnp.ones((16, SEQ), BF16)], axis=0)
        for j in range(ATT_Q_TILE // ATT_SUB_TILE):
            rows = slice(j * ATT_SUB_TILE, (j + 1) * ATT_SUB_TILE)
            st = lax.dot_general(k, q_ref[rows, :], (((1,), (1,)), ((), ())),
                                 preferred_element_type=F32)
            st_w[j] = st
            m_w[j] = jnp.max(st, axis=0, keepdims=True)
            p = jnp.exp2(st_r[j] - m_r[j]).astype(BF16)
            ot = jnp.dot(vt, p, preferred_element_type=F32)
            o_ref[rows, :] = (ot[:MLA_V, :] / ot[MLA_V:MLA_V + 1, :]).T.astype(o_ref.dtype)

    parity = lax.rem(t, 2)
    pl.when(parity == 0)(lambda: step(st_a, m_a, st_b, m_b))
    pl.when(parity == 1)(lambda: step(st_b, m_b, st_a, m_a))


def _attention(qf, kf, vt):
    t = qf.shape[0]
    nb = t // SEQ
    q_tiles = SEQ // ATT_Q_TILE
    n_tiles = nb * MLA_HEADS * q_tiles
    n_sub = ATT_Q_TILE // ATT_SUB_TILE

    def tile_of(u):
        return u // (MLA_HEADS * q_tiles), (u // q_tiles) % MLA_HEADS, u % q_tiles

    def cur(s):
        return tile_of(jnp.minimum(s, n_tiles - 1))

    def prv(s):
        return tile_of(jnp.maximum(s - 1, 0))

    def q_map(s):
        b, h, i = cur(s)
        return (b * q_tiles + i, h)

    def k_map(s):
        b, h, _ = cur(s)
        return (b, h)

    def vt_map(s):
        b, h, _ = prv(s)
        return (b, h, 0)

    def o_map(s):
        b, h, i = prv(s)
        return (b * q_tiles + i, h)

    score_buf = pltpu.VMEM((n_sub, SEQ, ATT_SUB_TILE), F32)
    max_buf = pltpu.VMEM((n_sub, 1, ATT_SUB_TILE), F32)
    return pl.pallas_call(
        _attention_kernel,
        grid=(n_tiles + 1,),
        in_specs=[
            pl.BlockSpec((ATT_Q_TILE, MLA_HEAD_PAD), q_map),
            pl.BlockSpec((SEQ, MLA_HEAD_PAD), k_map),
            pl.BlockSpec((None, MLA_V, SEQ), vt_map),
        ],
        out_specs=pl.BlockSpec((ATT_Q_TILE, MLA_V), o_map),
        out_shape=jax.ShapeDtypeStruct((t, MLA_V_W), BF16),
        scratch_shapes=[score_buf, max_buf, score_buf, max_buf],
        compiler_params=_cparams("arbitrary"),
        name="mla_attention",
    )(qf, kf, vt)


def _merge_kernel(x_ref, ron_ref, rg_ref, att_ref, ga_ref, gb_ref, g1_ref, gn_ref,
                  wro_ref, wmo_ref, wout_ref, lng_ref, lnb_ref, o_ref):
    rg = rg_ref[...].astype(F32)
    ya_in = (rg * jax.nn.sigmoid(rg)) * (ron_ref[...].astype(F32) * gn_ref[...])
    y_a = jnp.dot(ya_in.astype(BF16), wro_ref[...], preferred_element_type=F32)
    y_b = jnp.dot(att_ref[...], wmo_ref[...], preferred_element_type=F32)
    merged = (jax.nn.sigmoid(ga_ref[...].astype(F32)) * y_a
              + jax.nn.sigmoid(gb_ref[...].astype(F32)) * y_b)
    f = jnp.dot(merged.astype(BF16), wout_ref[...], preferred_element_type=F32)
    z = DEEPNORM_ALPHA * x_ref[...] + (1.0 + g1_ref[...]) * f
    o_ref[...] = _layernorm_rows(z) * lng_ref[...] + lnb_ref[...]


def _merge(x, ron, gates, att, g1, gn_g, w_ro, w_mo, w_out, ln_g, ln_b):
    t = x.shape[0]
    tiles_per_seq = SEQ // ROW_TILE
    row = lambda i: (i, 0)
    wide = pl.BlockSpec((ROW_TILE, D_MODEL), row)
    vec = _resident((1, D_MODEL))
    mat = _resident((D_MODEL, D_MODEL))
    return pl.pallas_call(
        _merge_kernel,
        grid=(t // ROW_TILE,),
        in_specs=[
            wide, wide,
            pl.BlockSpec((ROW_TILE, D_MODEL), lambda i: (i, 0)),
            wide,
            pl.BlockSpec((ROW_TILE, D_MODEL), lambda i: (i, (OFF_GA - OFF_RG) // D_MODEL)),
            pl.BlockSpec((ROW_TILE, D_MODEL), lambda i: (i, (OFF_GB - OFF_RG) // D_MODEL)),
            pl.BlockSpec((None, 1, D_MODEL), lambda i: (i // tiles_per_seq, 0, 0)),
            vec, mat, mat, mat, vec, vec,
        ],
        out_specs=wide,
        out_shape=jax.ShapeDtypeStruct((t, D_MODEL), F32),
        compiler_params=_cparams("parallel"),
        name="merge_out",
    )(x, ron, gates, att, gates, gates, g1, gn_g, w_ro, w_mo, w_out, ln_g, ln_b)


def _ffn_out_kernel(x_ref, a_ref, b_ref, ap_ref, an_ref, g2_ref, cw_ref, cb_ref, wd_ref,
                    lng_ref, lnb_ref, o_ref, *, first_tile):
    tiles_per_seq = SEQ // ROW_TILE
    i = pl.program_id(0) + first_tile
    tm = ROW_TILE
    first = (i % tiles_per_seq) == 0
    last = (i % tiles_per_seq) == tiles_per_seq - 1
    inv_sqrt2 = 1.0 / math.sqrt(2.0)
    cw = cw_ref[...] * inv_sqrt2
    cb = cb_ref[...] * inv_sqrt2
    prev_row = jnp.where(first, 0.0, ap_ref[7:8, :].astype(F32))
    next_row = jnp.where(last, 0.0, an_ref[0:1, :].astype(F32))
    r = lax.broadcasted_iota(jnp.int32, (tm, FFN_K_TILE), 0)
    y = None
    for c in range(D_FF // FFN_K_TILE):
        cols = slice(c * FFN_K_TILE, (c + 1) * FFN_K_TILE)
        a = a_ref[:, cols].astype(F32)
        a_up = jnp.where(r == 0, prev_row[:, cols], pltpu.roll(a, 1, axis=0))
        a_dn = jnp.where(r == tm - 1, next_row[:, cols], pltpu.roll(a, tm - 1, axis=0))
        conv = a_up * cw[0:1, cols] + a * cw[1:2, cols] + a_dn * cw[2:3, cols] + cb[:, cols]
        hmid = (conv * (1.0 + lax.erf(conv))).astype(BF16) * b_ref[:, cols]
        part = jnp.dot(hmid, wd_ref[cols, :], preferred_element_type=F32)
        y = part if y is None else y + part
    z = DEEPNORM_ALPHA * x_ref[...] + ((1.0 + g2_ref[...]) * inv_sqrt2) * y
    o_ref[...] = _layernorm_rows(z) * lng_ref[...] + lnb_ref[...]


def _ffn_out(x, ua, ub, g2, conv_w, conv_b, w_down, ln_g, ln_b, first_tile, n_tiles):
    tiles_per_seq = SEQ // ROW_TILE
    halo_per_tile = ROW_TILE // 8
    n_halo = x.shape[0] // 8
    row = lambda i: (i + first_tile, 0)
    return pl.pallas_call(
        functools.partial(_ffn_out_kernel, first_tile=first_tile),
        grid=(n_tiles,),
        in_specs=[
            pl.BlockSpec((ROW_TILE, D_MODEL), row),
            pl.BlockSpec((ROW_TILE, D_FF), lambda i: (i + first_tile, 0)),
            pl.BlockSpec((ROW_TILE, D_FF), lambda i: (i + first_tile, 0)),
            pl.BlockSpec((8, D_FF), lambda i: (jnp.maximum((i + first_tile) * halo_per_tile - 1, 0), 0)),
            pl.BlockSpec((8, D_FF),
                         lambda i: (jnp.minimum((i + first_tile + 1) * halo_per_tile, n_halo - 1), 0)),
            pl.BlockSpec((None, 1, D_MODEL), lambda i: ((i + first_tile) // tiles_per_seq, 0, 0)),
            _resident((3, D_FF)),
            _resident((1, D_FF)),
            _resident((D_FF, D_MODEL)),
            _resident((1, D_MODEL)),
            _resident((1, D_MODEL)),
        ],
        out_specs=pl.BlockSpec((ROW_TILE, D_MODEL), lambda i: (i, 0)),
        out_shape=jax.ShapeDtypeStruct((n_tiles * ROW_TILE, D_MODEL), F32),
        compiler_params=_cparams("parallel"),
        name="ffn_out",
    )(x, ua, ub, ua, ua, g2, conv_w, conv_b, w_down, ln_g, ln_b)


def _rope_cos_sin(dim):
    inv = 1.0 / (ROPE_THETA ** (jnp.arange(0, dim, 2, dtype=F32) / dim))
    ang = jnp.arange(SEQ, dtype=F32)[:, None] * inv[None, :]
    return jnp.cos(ang), jnp.sin(ang)


def _swap_halves(w):
    half = w.shape[-1] // 2
    return jnp.concatenate([w[..., half:], w[..., :half]], axis=-1)


def _prep_w_in(w_in):
    rq, rk, rv, rg, dq, dkv, kr, ga, gb = jnp.split(
        w_in, [512, 1024, 2048, 3072, 3328, 3456, 3520, 4544], axis=-1)
    return jnp.concatenate([rq, rk, rv, rg, ga, gb, dq, dkv, kr, _swap_halves(kr)], axis=-1).astype(BF16)


def _prep_w_uq(w_uq):
    w = w_uq.reshape(DEPTH, Q_LORA, MLA_HEADS, MLA_QK)
    rope = w[..., MLA_NOPE:]
    w = jnp.concatenate([w[..., :MLA_NOPE], rope, _swap_halves(rope)], axis=-1)
    return w.reshape(DEPTH, Q_LORA, MLA_HEADS * MLA_HEAD_PAD).astype(BF16)


def kernel(x_prompt, x_sample, c_prompt, c_sample, w_ada, b_ada, w_in, ret_decay_fwd, ret_decay_bwd,
           ret_gn_g, w_ret_o, q_norm_g, kv_norm_g, w_uq, w_uk, w_uv, w_mla_o, w_out, ln1_g, ln1_b,
           w_up, conv_w, conv_b, w_down, ln2_g, ln2_b):
    nb_p = x_prompt.shape[0]
    x = jnp.concatenate([x_prompt, x_sample], axis=0)
    nb = x.shape[0]
    x = x.reshape(nb * SEQ, D_MODEL)
    n_tiles = nb * SEQ // ROW_TILE
    tiles_p = nb_p * SEQ // ROW_TILE
    c = jnp.concatenate([c_prompt, c_sample], axis=0)

    ada = _ada_table(c, w_ada, b_ada).reshape(DEPTH, 6, nb, 1, D_MODEL)

    cos_r, sin_r = _rope_cos_sin(RET_QK_DIM)
    cos_r = jnp.concatenate([cos_r, cos_r], axis=-1)
    sin_r = jnp.concatenate([-sin_r, sin_r], axis=-1)
    cos_m, sin_m = _rope_cos_sin(MLA_ROPE)
    cs_m = jnp.concatenate([cos_m, cos_m, -sin_m, sin_m], axis=-1)

    w_in_e = _prep_w_in(w_in)
    w_uq_e = _prep_w_uq(w_uq)
    w_ukv = jnp.concatenate([w_uk, w_uv], axis=-1).astype(BF16)
    w_ro, w_mo, w_o = w_ret_o.astype(BF16), w_mla_o.astype(BF16), w_out.astype(BF16)
    w_u, w_d = w_up.astype(BF16), w_down.astype(BF16)
    dec_f = jnp.broadcast_to(ret_decay_fwd[:, :, None, None], (DEPTH, RET_HEADS, 1, RET_V_DIM))
    dec_b = jnp.broadcast_to(ret_decay_bwd[:, :, None, None], (DEPTH, RET_HEADS, 1, RET_V_DIM))

    for l in range(DEPTH):
        sh1, sc1, g1, sh2, sc2, g2 = (ada[l, k] for k in range(6))
        qkv, gates, qf, kf, vt = _in_proj(x, sc1, sh1, w_in_e[l], cs_m, cos_r, sin_r, q_norm_g[l][None],
                                          kv_norm_g[l][None], w_uq_e[l], w_ukv[l])
        ron = _retention(qkv, dec_f[l], dec_b[l])
        att = _attention(qf, kf, vt)
        x = _merge(x, ron, gates, att, g1, ret_gn_g[l][None], w_ro[l], w_mo[l], w_o[l],
                   ln1_g[l][None], ln1_b[l][None])
        ua, ub = _up_proj(x, sc2, sh2, w_u[l])
        ffn_out = functools.partial(_ffn_out, x, ua, ub, g2, conv_w[l], conv_b[l][None], w_d[l],
                                    ln2_g[l][None], ln2_b[l][None])
        if l < DEPTH - 1:
            x = ffn_out(0, n_tiles)
        else:
            y_prompt = ffn_out(0, tiles_p)
            y_sample = ffn_out(tiles_p, n_tiles - tiles_p)

    return (y_prompt.reshape(nb_p, SEQ, D_MODEL), y_sample.reshape(nb - nb_p, SEQ, D_MODEL))
```

```python
import functools
import math

import jax
import jax.numpy as jnp
from jax import lax
from jax.experimental import pallas as pl
from jax.experimental.pallas import tpu as pltpu

D_MODEL = 1024
SEQ = 2048
DEPTH = 4
RET_HEADS = 4
RET_QK_DIM = 128
RET_V_DIM = 256
RET_QK_W = RET_HEADS * RET_QK_DIM
RET_V_W = RET_HEADS * RET_V_DIM
RET_CHUNK = 128
MLA_HEADS = 8
MLA_NOPE = 128
MLA_ROPE = 64
MLA_V = 128
Q_LORA = 256
KV_LORA = 128
MLA_QK = MLA_NOPE + MLA_ROPE
MLA_V_W = MLA_HEADS * MLA_V
MLA_HEAD_PAD = 256
D_FF = 2816
ROPE_THETA = 10000.0
LN_EPS = 1e-5
RMS_EPS = 1e-6
DEEPNORM_ALPHA = (2.0 * DEPTH) ** 0.25

IN_EXT_W = 5632
OFF_RQ, OFF_RK, OFF_RV, OFF_RG, OFF_GA, OFF_GB = 0, 512, 1024, 2048, 3072, 4096
OFF_DQ, OFF_DKV, OFF_KR = 5120, 5376, 5504

V7X_VMEM_LIMIT_BYTES = 56 * 1024 * 1024
ROW_TILE = 512
UP_ROW_TILE = 1024
MM_COL_TILE = 512
FFN_K_TILE = 256
ATT_Q_TILE = 2048
ATT_SUB_TILE = 512
LOG2_E = 1.4426950408889634

BF16 = jnp.bfloat16
F32 = jnp.float32


def _cparams(*sem):
    return pltpu.CompilerParams(dimension_semantics=sem, vmem_limit_bytes=V7X_VMEM_LIMIT_BYTES)


def _resident(block_shape):
    return pl.BlockSpec(block_shape, lambda *_: (0,) * len(block_shape), pipeline_mode=pl.Buffered(1))


def _layernorm_rows(x):
    mu = jnp.mean(x, axis=-1, keepdims=True)
    xc = x - mu
    var = jnp.mean(xc * xc, axis=-1, keepdims=True)
    return xc * lax.rsqrt(var + LN_EPS)


def _ada_kernel(c_ref, w_ref, b_ref, o_ref):
    c = c_ref[...]
    s = (c * jax.nn.sigmoid(c)).astype(BF16)
    o_ref[...] = jnp.dot(s, w_ref[...].astype(BF16), preferred_element_type=F32) + b_ref[...]


def _ada_table(c, w_ada, b_ada):
    nb = c.shape[0]
    return pl.pallas_call(
        _ada_kernel,
        grid=(DEPTH, 6),
        in_specs=[
            pl.BlockSpec((nb, D_MODEL), lambda l, j: (0, 0)),
            pl.BlockSpec((None, D_MODEL, D_MODEL), lambda l, j: (l, 0, j)),
            pl.BlockSpec((None, 1, D_MODEL), lambda l, j: (l, 0, j)),
        ],
        out_specs=pl.BlockSpec((None, None, nb, D_MODEL), lambda l, j: (l, j, 0, 0)),
        out_shape=jax.ShapeDtypeStruct((DEPTH, 6, nb, D_MODEL), F32),
        compiler_params=_cparams("parallel", "parallel"),
        name="ada_table",
    )(c, w_ada, b_ada.reshape(DEPTH, 1, 6 * D_MODEL))


def _col_chunks(width):
    return [(c, min(MM_COL_TILE, width - c)) for c in range(0, width, MM_COL_TILE)]


def _up_proj_kernel(x_ref, sc_ref, sh_ref, w_ref, a_ref, b_ref, h_a, h_b):
    i = pl.program_id(0)

    @pl.when(i == 0)
    def _():
        h_b[...] = jnp.zeros(h_b.shape, BF16)

    def step(h_prev, h_cur):
        for c, wd in _col_chunks(D_FF):
            b_ref[:, c:c + wd] = jnp.dot(h_prev[...], w_ref[:, D_FF + c:D_FF + c + wd],
                                         preferred_element_type=F32).astype(b_ref.dtype)
        hb = (_layernorm_rows(x_ref[...]) * (1.0 + sc_ref[...]) + sh_ref[...]).astype(BF16)
        h_cur[...] = hb
        for c, wd in _col_chunks(D_FF):
            a_ref[:, c:c + wd] = jnp.dot(hb, w_ref[:, c:c + wd], preferred_element_type=F32).astype(a_ref.dtype)

    parity = lax.rem(i, 2)
    pl.when(parity == 0)(lambda: step(h_b, h_a))
    pl.when(parity == 1)(lambda: step(h_a, h_b))


def _up_proj(x, sc, sh, w):
    t = x.shape[0]
    n_tiles = t // UP_ROW_TILE
    tiles_per_seq = SEQ // UP_ROW_TILE
    cur = lambda i: jnp.minimum(i, n_tiles - 1)
    prv = lambda i: jnp.maximum(i - 1, 0)
    mod_spec = pl.BlockSpec((None, 1, D_MODEL), lambda i: (cur(i) // tiles_per_seq, 0, 0))
    h_buf = pltpu.VMEM((UP_ROW_TILE, D_MODEL), BF16)
    return pl.pallas_call(
        _up_proj_kernel,
        grid=(n_tiles + 1,),
        in_specs=[
            pl.BlockSpec((UP_ROW_TILE, D_MODEL), lambda i: (cur(i), 0)),
            mod_spec, mod_spec,
            _resident((D_MODEL, 2 * D_FF)),
        ],
        out_specs=[
            pl.BlockSpec((UP_ROW_TILE, D_FF), lambda i: (cur(i), 0)),
            pl.BlockSpec((UP_ROW_TILE, D_FF), lambda i: (prv(i), 0)),
        ],
        out_shape=[jax.ShapeDtypeStruct((t, D_FF), BF16), jax.ShapeDtypeStruct((t, D_FF), BF16)],
        scratch_shapes=[h_buf, h_buf],
        compiler_params=_cparams("arbitrary"),
        name="up_proj",
    )(x, sc, sh, w)


def _rms_rows(x, g):
    return x * lax.rsqrt(jnp.mean(x * x, axis=-1, keepdims=True) + RMS_EPS) * g


def _tile_rows(x_refs, i, split_tile):
    if split_tile is None:
        return x_refs[0][...]
    return jnp.where(i < split_tile, x_refs[0][...], x_refs[1][...])


def _tile_specs(xs, tile_of_step, split_tile):
    if split_tile is None:
        return [pl.BlockSpec((ROW_TILE, D_MODEL), lambda i: (tile_of_step(i), 0))]
    return [pl.BlockSpec((ROW_TILE, D_MODEL), lambda i: (jnp.minimum(tile_of_step(i), split_tile - 1), 0)),
            pl.BlockSpec((ROW_TILE, D_MODEL), lambda i: (jnp.maximum(tile_of_step(i) - split_tile, 0), 0))]


def _in_proj_kernel(*refs, split_tile):
    n_x = 1 if split_tile is None else 2
    x_refs = refs[:n_x]
    (sc_ref, sh_ref, w_ref, cs_ref, cos_ref, sin_ref, qg_ref, kvg_ref, wq_ref, wkv_ref,
     lo_ref, hi_ref, qf_ref, kf_ref, vt_ref, h_a, h_b) = refs[n_x:]
    i = pl.program_id(0)

    @pl.when(i == 0)
    def _():
        h_b[...] = jnp.zeros(h_b.shape, BF16)

    def step(h_prev, h_cur):
        for c, wd in _col_chunks(OFF_DQ - OFF_RG):
            hi_ref[:, c:c + wd] = jnp.dot(h_prev[...], w_ref[:, OFF_RG + c:OFF_RG + c + wd],
                                          preferred_element_type=F32).astype(hi_ref.dtype)
        x = _tile_rows(x_refs, i, split_tile)
        hb = (_layernorm_rows(x) * (1.0 + sc_ref[...]) + sh_ref[...]).astype(BF16)
        h_cur[...] = hb
        lat = jnp.dot(hb, w_ref[:, OFF_DQ:IN_EXT_W], preferred_element_type=F32)

        cos = cos_ref[...]
        sin = sin_ref[...]
        for c, wd in _col_chunks(OFF_RG):
            res = jnp.dot(hb, w_ref[:, c:c + wd], preferred_element_type=F32)
            if c < OFF_RV:
                gain = 1.0 if c < OFF_RK else RET_QK_DIM ** -0.5
                for h0 in range(0, wd, RET_QK_DIM):
                    xh = res[:, h0:h0 + RET_QK_DIM]
                    roped = (xh * cos + pltpu.roll(xh, RET_QK_DIM // 2, axis=1) * sin) * gain
                    lo_ref[:, c + h0:c + h0 + RET_QK_DIM] = roped.astype(lo_ref.dtype)
            else:
                lo_ref[:, c:c + wd] = res.astype(lo_ref.dtype)

        scale = MLA_QK ** -0.5 * LOG2_E
        cs = cs_ref[...]
        cq = _rms_rows(lat[:, 0:Q_LORA], qg_ref[...]).astype(BF16)
        ckv = _rms_rows(lat[:, Q_LORA:Q_LORA + KV_LORA], kvg_ref[...]).astype(BF16)
        tk = lat[:, Q_LORA + KV_LORA:] * cs
        lane = lax.broadcasted_iota(jnp.int32, tk.shape, 1)
        k_rope = jnp.where(lane < MLA_ROPE, tk + pltpu.roll(tk, MLA_ROPE, axis=1), 0.0).astype(BF16)
        kv = jnp.dot(ckv, wkv_ref[...], preferred_element_type=F32)
        vt_ref[...] = kv[:, MLA_V_W:].T.astype(BF16)
        for h in range(MLA_HEADS):
            base = h * MLA_HEAD_PAD
            qh = jnp.dot(cq, wq_ref[:, base:base + MLA_HEAD_PAD], preferred_element_type=F32)
            tq = qh[:, MLA_NOPE:] * cs
            q_rope = tq + pltpu.roll(tq, MLA_ROPE, axis=1)
            qf_ref[h, :, :MLA_NOPE] = (qh[:, :MLA_NOPE] * scale).astype(BF16)
            qf_ref[h, :, MLA_NOPE:] = (q_rope * scale).astype(BF16)
            kf_ref[h, :, :MLA_NOPE] = kv[:, h * MLA_NOPE:(h + 1) * MLA_NOPE].astype(BF16)
            kf_ref[h, :, MLA_NOPE:] = k_rope

    parity = lax.rem(i, 2)
    pl.when(parity == 0)(lambda: step(h_b, h_a))
    pl.when(parity == 1)(lambda: step(h_a, h_b))


def _in_proj(xs, sc, sh, w_in_e, cs_m, cos_r, sin_r, q_norm_g, kv_norm_g, wq_ext, wkv):
    t = sum(x.shape[0] for x in xs)
    split_tile = None if len(xs) == 1 else xs[0].shape[0] // ROW_TILE
    n_tiles = t // ROW_TILE
    tiles_per_seq = SEQ // ROW_TILE
    wide = MLA_HEADS * MLA_HEAD_PAD
    cur = lambda i: jnp.minimum(i, n_tiles - 1)
    prv = lambda i: jnp.maximum(i - 1, 0)
    row = lambda i: (cur(i), 0)
    pos = lambda i: (cur(i) % tiles_per_seq, 0)
    mod_spec = pl.BlockSpec((None, 1, D_MODEL), lambda i: (cur(i) // tiles_per_seq, 0, 0))
    h_buf = pltpu.VMEM((ROW_TILE, D_MODEL), BF16)
    head_major = pl.BlockSpec((None, MLA_HEADS, ROW_TILE, MLA_HEAD_PAD),
                              lambda i: (cur(i) // tiles_per_seq, 0, cur(i) % tiles_per_seq, 0))
    return pl.pallas_call(
        functools.partial(_in_proj_kernel, split_tile=split_tile),
        grid=(n_tiles + 1,),
        in_specs=_tile_specs(xs, cur, split_tile) + [
            mod_spec, mod_spec,
            _resident((D_MODEL, IN_EXT_W)),
            pl.BlockSpec((ROW_TILE, 128), pos),
            pl.BlockSpec((ROW_TILE, RET_QK_DIM), pos),
            pl.BlockSpec((ROW_TILE, RET_QK_DIM), pos),
            _resident((1, Q_LORA)),
            _resident((1, KV_LORA)),
            _resident((Q_LORA, wide)),
            _resident((KV_LORA, 2 * MLA_V_W)),
        ],
        out_specs=[
            pl.BlockSpec((ROW_TILE, OFF_RG), row),
            pl.BlockSpec((ROW_TILE, OFF_DQ - OFF_RG), lambda i: (prv(i), 0)),
            head_major, head_major,
            pl.BlockSpec((None, MLA_V_W, ROW_TILE), lambda i: (cur(i) // tiles_per_seq, 0, cur(i) % tiles_per_seq)),
        ],
        out_shape=[
            jax.ShapeDtypeStruct((t, OFF_RG), BF16),
            jax.ShapeDtypeStruct((t, OFF_DQ - OFF_RG), BF16),
            jax.ShapeDtypeStruct((t // SEQ, MLA_HEADS, SEQ, MLA_HEAD_PAD), BF16),
            jax.ShapeDtypeStruct((t // SEQ, MLA_HEADS, SEQ, MLA_HEAD_PAD), BF16),
            jax.ShapeDtypeStruct((t // SEQ, MLA_V_W, SEQ), BF16),
        ],
        scratch_shapes=[h_buf, h_buf],
        compiler_params=_cparams("arbitrary"),
        name="in_proj",
    )(*xs, sc, sh, w_in_e, cs_m, cos_r, sin_r, q_norm_g, kv_norm_g, wq_ext, wkv)


def _retention_kernel(q_ref, k_ref, v_ref, df_ref, db_ref, o_ref, lhs_ref, rhs_ref, kvf_ref, kvb_ref):
    c = RET_CHUNK
    n_chunks = SEQ // c
    dk = RET_QK_DIM
    q = q_ref[...].astype(F32)
    kt = k_ref[...].astype(F32).T

    lg_f = jnp.log(jax.nn.sigmoid(df_ref[...]))
    lg_b = jnp.log(jax.nn.sigmoid(db_ref[...]))
    lgf = lg_f[:, :c]
    lgb = lg_b[:, :c]
    row = lax.broadcasted_iota(jnp.int32, (c, c), 0).astype(F32)
    col = lax.broadcasted_iota(jnp.int32, (c, c), 1).astype(F32)
    diff = row - col
    dmat = jnp.where(diff >= 0, jnp.exp(lgf * jnp.maximum(diff, 0.0)),
                     jnp.exp(lgb * jnp.maximum(-diff, 0.0)))
    qd_f = jnp.exp(lgf * (row + 1.0))
    qd_b = jnp.exp(lgb * (c - row))
    pos = col[0:1, :]
    kd_f = jnp.exp(lgf * (c - 1.0 - pos))
    kd_b = jnp.exp(lgb * pos)
    cd_f = jnp.exp(lg_f * float(c))
    cd_b = jnp.exp(lg_b * float(c))

    for n in range(n_chunks):
        rows = slice(n * c, (n + 1) * c)
        qn = q[rows, :]
        ktn = kt[:, rows]
        vn = v_ref[rows, :]
        s = jnp.dot(qn.astype(BF16), ktn.astype(BF16), preferred_element_type=F32) * dmat
        lhs_ref[rows, 0:c] = s.astype(BF16)
        lhs_ref[rows, c:c + dk] = (qn * qd_f).astype(BF16)
        lhs_ref[rows, c + dk:c + 2 * dk] = (qn * qd_b).astype(BF16)
        rhs_ref[n, 0:c, :] = vn
        kvf_ref[n] = jnp.dot((ktn * kd_f).astype(BF16), vn, preferred_element_type=F32)
        kvb_ref[n] = jnp.dot((ktn * kd_b).astype(BF16), vn, preferred_element_type=F32)

    state = jnp.zeros((dk, RET_V_DIM), F32)
    for n in range(n_chunks):
        rhs_ref[n, c:c + dk, :] = state.astype(BF16)
        state = cd_f * state + kvf_ref[n]
    state = jnp.zeros((dk, RET_V_DIM), F32)
    for n in reversed(range(n_chunks)):
        rhs_ref[n, c + dk:c + 2 * dk, :] = state.astype(BF16)
        state = cd_b * state + kvb_ref[n]

    for n in range(n_chunks):
        rows = slice(n * c, (n + 1) * c)
        o = jnp.dot(lhs_ref[rows, :], rhs_ref[n], preferred_element_type=F32)
        o_ref[rows, :] = _layernorm_rows(o).astype(o_ref.dtype)


def _retention(proj, dec_f, dec_b):
    t = proj.shape[0]
    nb = t // SEQ
    dec = pl.BlockSpec((None, 1, RET_V_DIM), lambda b, h: (h, 0, 0))
    return pl.pallas_call(
        _retention_kernel,
        grid=(nb, RET_HEADS),
        in_specs=[
            pl.BlockSpec((SEQ, RET_QK_DIM), lambda b, h: (b, OFF_RQ // RET_QK_DIM + h)),
            pl.BlockSpec((SEQ, RET_QK_DIM), lambda b, h: (b, OFF_RK // RET_QK_DIM + h)),
            pl.BlockSpec((SEQ, RET_V_DIM), lambda b, h: (b, OFF_RV // RET_V_DIM + h)),
            dec, dec,
        ],
        out_specs=pl.BlockSpec((SEQ, RET_V_DIM), lambda b, h: (b, h)),
        out_shape=jax.ShapeDtypeStruct((t, RET_V_W), BF16),
        scratch_shapes=[
            pltpu.VMEM((SEQ, RET_CHUNK + 2 * RET_QK_DIM), BF16),
            pltpu.VMEM((SEQ // RET_CHUNK, RET_CHUNK + 2 * RET_QK_DIM, RET_V_DIM), BF16),
            pltpu.VMEM((SEQ // RET_CHUNK, RET_QK_DIM, RET_V_DIM), F32),
            pltpu.VMEM((SEQ // RET_CHUNK, RET_QK_DIM, RET_V_DIM), F32),
        ],
        compiler_params=_cparams("parallel", "parallel"),
        name="retention",
    )(proj, proj, proj, dec_f, dec_b)


def _attention_kernel(q_ref, k_ref, vt_ref, o_ref, st_a, m_a, st_b, m_b):
    t = pl.program_id(0)

    @pl.when(t == 0)
    def _():
        st_b[...] = jnp.zeros(st_b.shape, F32)
        m_b[...] = jnp.zeros(m_b.shape, F32)

    def step(st_w, m_w, st_r, m_r):
        k = k_ref[...]
        vt = jnp.concatenate([vt_ref[...], jnp.ones((16, SEQ), BF16)], axis=0)
        for j in range(ATT_Q_TILE // ATT_SUB_TILE):
            rows = slice(j * ATT_SUB_TILE, (j + 1) * ATT_SUB_TILE)
            st = lax.dot_general(k, q_ref[rows, :], (((1,), (1,)), ((), ())),
                                 preferred_element_type=F32)
            st_w[j] = st
            m_w[j] = jnp.max(st, axis=0, keepdims=True)
            p = jnp.exp2(st_r[j] - m_r[j]).astype(BF16)
            ot = jnp.dot(vt, p, preferred_element_type=F32)
            o_ref[rows, :] = (ot[:MLA_V, :] / ot[MLA_V:MLA_V + 1, :]).T.astype(o_ref.dtype)

    parity = lax.rem(t, 2)
    pl.when(parity == 0)(lambda: step(st_a, m_a, st_b, m_b))
    pl.when(parity == 1)(lambda: step(st_b, m_b, st_a, m_a))


def _attention(qf, kf, vt):
    nb = qf.shape[0]
    q_tiles = SEQ // ATT_Q_TILE
    n_tiles = nb * MLA_HEADS * q_tiles
    n_sub = ATT_Q_TILE // ATT_SUB_TILE

    def tile_of(u):
        return u // (MLA_HEADS * q_tiles), (u // q_tiles) % MLA_HEADS, u % q_tiles

    def cur(s):
        return tile_of(jnp.minimum(s, n_tiles - 1))

    def prv(s):
        return tile_of(jnp.maximum(s - 1, 0))

    def q_map(s):
        b, h, i = cur(s)
        return (b, h, i, 0)

    def k_map(s):
        b, h, _ = cur(s)
        return (b, h, 0, 0)

    def vt_map(s):
        b, h, _ = prv(s)
        return (b, h, 0)

    def o_map(s):
        b, h, i = prv(s)
        return (b, h, i, 0)

    score_buf = pltpu.VMEM((n_sub, SEQ, ATT_SUB_TILE), F32)
    max_buf = pltpu.VMEM((n_sub, 1, ATT_SUB_TILE), F32)
    return pl.pallas_call(
        _attention_kernel,
        grid=(n_tiles + 1,),
        in_specs=[
            pl.BlockSpec((None, None, ATT_Q_TILE, MLA_HEAD_PAD), q_map),
            pl.BlockSpec((None, None, SEQ, MLA_HEAD_PAD), k_map),
            pl.BlockSpec((None, MLA_V, SEQ), vt_map),
        ],
        out_specs=pl.BlockSpec((None, None, ATT_Q_TILE, MLA_V), o_map),
        out_shape=jax.ShapeDtypeStruct((nb, MLA_HEADS, SEQ, MLA_V), BF16),
        scratch_shapes=[score_buf, max_buf, score_buf, max_buf],
        compiler_params=_cparams("arbitrary"),
        name="mla_attention",
    )(qf, kf, vt)


def _merge_kernel(*refs, split_tile):
    n_x = 1 if split_tile is None else 2
    x_refs = refs[:n_x]
    (ron_ref, rg_ref, att_ref, ga_ref, gb_ref, g1_ref, gn_ref,
     wro_ref, wmo_ref, wout_ref, lng_ref, lnb_ref, o_ref) = refs[n_x:]
    rg = rg_ref[...].astype(F32)
    ya_in = (rg * jax.nn.sigmoid(rg)) * (ron_ref[...].astype(F32) * gn_ref[...])
    y_a = jnp.dot(ya_in.astype(BF16), wro_ref[...], preferred_element_type=F32)
    att = jnp.concatenate([att_ref[h] for h in range(MLA_HEADS)], axis=-1)
    y_b = jnp.dot(att, wmo_ref[...], preferred_element_type=F32)
    merged = (jax.nn.sigmoid(ga_ref[...].astype(F32)) * y_a
              + jax.nn.sigmoid(gb_ref[...].astype(F32)) * y_b)
    f = jnp.dot(merged.astype(BF16), wout_ref[...], preferred_element_type=F32)
    z = DEEPNORM_ALPHA * _tile_rows(x_refs, pl.program_id(0), split_tile) + (1.0 + g1_ref[...]) * f
    o_ref[...] = _layernorm_rows(z) * lng_ref[...] + lnb_ref[...]


def _merge(xs, ron, gates, att, g1, gn_g, w_ro, w_mo, w_out, ln_g, ln_b):
    t = sum(x.shape[0] for x in xs)
    split_tile = None if len(xs) == 1 else xs[0].shape[0] // ROW_TILE
    tiles_per_seq = SEQ // ROW_TILE
    row = lambda i: (i, 0)
    wide = pl.BlockSpec((ROW_TILE, D_MODEL), row)
    vec = _resident((1, D_MODEL))
    mat = _resident((D_MODEL, D_MODEL))
    return pl.pallas_call(
        functools.partial(_merge_kernel, split_tile=split_tile),
        grid=(t // ROW_TILE,),
        in_specs=_tile_specs(xs, lambda i: i, split_tile) + [
            wide,
            pl.BlockSpec((ROW_TILE, D_MODEL), lambda i: (i, 0)),
            pl.BlockSpec((None, MLA_HEADS, ROW_TILE, MLA_V), lambda i: (i // tiles_per_seq, 0, i % tiles_per_seq, 0)),
            pl.BlockSpec((ROW_TILE, D_MODEL), lambda i: (i, (OFF_GA - OFF_RG) // D_MODEL)),
            pl.BlockSpec((ROW_TILE, D_MODEL), lambda i: (i, (OFF_GB - OFF_RG) // D_MODEL)),
            pl.BlockSpec((None, 1, D_MODEL), lambda i: (i // tiles_per_seq, 0, 0)),
            vec, mat, mat, mat, vec, vec,
        ],
        out_specs=wide,
        out_shape=jax.ShapeDtypeStruct((t, D_MODEL), F32),
        compiler_params=_cparams("parallel"),
        name="merge_out",
    )(*xs, ron, gates, att, gates, gates, g1, gn_g, w_ro, w_mo, w_out, ln_g, ln_b)


def _ffn_out_kernel(x_ref, a_ref, b_ref, ap_ref, an_ref, g2_ref, cw_ref, cb_ref, wd_ref,
                    lng_ref, lnb_ref, o_ref, *, first_tile):
    tiles_per_seq = SEQ // ROW_TILE
    i = pl.program_id(0) + first_tile
    tm = ROW_TILE
    first = (i % tiles_per_seq) == 0
    last = (i % tiles_per_seq) == tiles_per_seq - 1
    inv_sqrt2 = 1.0 / math.sqrt(2.0)
    cw = cw_ref[...] * inv_sqrt2
    cb = cb_ref[...] * inv_sqrt2
    prev_row = jnp.where(first, 0.0, ap_ref[7:8, :].astype(F32))
    next_row = jnp.where(last, 0.0, an_ref[0:1, :].astype(F32))
    r = lax.broadcasted_iota(jnp.int32, (tm, FFN_K_TILE), 0)
    y = None
    for c in range(D_FF // FFN_K_TILE):
        cols = slice(c * FFN_K_TILE, (c + 1) * FFN_K_TILE)
        a = a_ref[:, cols].astype(F32)
        a_up = jnp.where(r == 0, prev_row[:, cols], pltpu.roll(a, 1, axis=0))
        a_dn = jnp.where(r == tm - 1, next_row[:, cols], pltpu.roll(a, tm - 1, axis=0))
        conv = a_up * cw[0:1, cols] + a * cw[1:2, cols] + a_dn * cw[2:3, cols] + cb[:, cols]
        hmid = (conv * (1.0 + lax.erf(conv))).astype(BF16) * b_ref[:, cols]
        part = jnp.dot(hmid, wd_ref[cols, :], preferred_element_type=F32)
        y = part if y is None else y + part
    z = DEEPNORM_ALPHA * x_ref[...] + ((1.0 + g2_ref[...]) * inv_sqrt2) * y
    o_ref[...] = _layernorm_rows(z) * lng_ref[...] + lnb_ref[...]


def _ffn_out(x, ua, ub, g2, conv_w, conv_b, w_down, ln_g, ln_b, first_tile, n_tiles):
    tiles_per_seq = SEQ // ROW_TILE
    halo_per_tile = ROW_TILE // 8
    n_halo = x.shape[0] // 8
    row = lambda i: (i + first_tile, 0)
    return pl.pallas_call(
        functools.partial(_ffn_out_kernel, first_tile=first_tile),
        grid=(n_tiles,),
        in_specs=[
            pl.BlockSpec((ROW_TILE, D_MODEL), row),
            pl.BlockSpec((ROW_TILE, D_FF), lambda i: (i + first_tile, 0)),
            pl.BlockSpec((ROW_TILE, D_FF), lambda i: (i + first_tile, 0)),
            pl.BlockSpec((8, D_FF), lambda i: (jnp.maximum((i + first_tile) * halo_per_tile - 1, 0), 0)),
            pl.BlockSpec((8, D_FF),
                         lambda i: (jnp.minimum((i + first_tile + 1) * halo_per_tile, n_halo - 1), 0)),
            pl.BlockSpec((None, 1, D_MODEL), lambda i: ((i + first_tile) // tiles_per_seq, 0, 0)),
            _resident((3, D_FF)),
            _resident((1, D_FF)),
            _resident((D_FF, D_MODEL)),
            _resident((1, D_MODEL)),
            _resident((1, D_MODEL)),
        ],
        out_specs=pl.BlockSpec((ROW_TILE, D_MODEL), lambda i: (i, 0)),
        out_shape=jax.ShapeDtypeStruct((n_tiles * ROW_TILE, D_MODEL), F32),
        compiler_params=_cparams("parallel"),
        name="ffn_out",
    )(x, ua, ub, ua, ua, g2, conv_w, conv_b, w_down, ln_g, ln_b)


def _rope_cos_sin(dim):
    inv = 1.0 / (ROPE_THETA ** (jnp.arange(0, dim, 2, dtype=F32) / dim))
    ang = jnp.arange(SEQ, dtype=F32)[:, None] * inv[None, :]
    return jnp.cos(ang), jnp.sin(ang)


def _swap_halves(w):
    half = w.shape[-1] // 2
    return jnp.concatenate([w[..., half:], w[..., :half]], axis=-1)


def _prep_w_in(w_in):
    rq, rk, rv, rg, dq, dkv, kr, ga, gb = jnp.split(
        w_in, [512, 1024, 2048, 3072, 3328, 3456, 3520, 4544], axis=-1)
    return jnp.concatenate([rq, rk, rv, rg, ga, gb, dq, dkv, kr, _swap_halves(kr)], axis=-1).astype(BF16)


def _prep_w_uq(w_uq):
    w = w_uq.reshape(DEPTH, Q_LORA, MLA_HEADS, MLA_QK)
    rope = w[..., MLA_NOPE:]
    w = jnp.concatenate([w[..., :MLA_NOPE], rope, _swap_halves(rope)], axis=-1)
    return w.reshape(DEPTH, Q_LORA, MLA_HEADS * MLA_HEAD_PAD).astype(BF16)


def kernel(x_prompt, x_sample, c_prompt, c_sample, w_ada, b_ada, w_in, ret_decay_fwd, ret_decay_bwd,
           ret_gn_g, w_ret_o, q_norm_g, kv_norm_g, w_uq, w_uk, w_uv, w_mla_o, w_out, ln1_g, ln1_b,
           w_up, conv_w, conv_b, w_down, ln2_g, ln2_b):
    nb_p = x_prompt.shape[0]
    nb = nb_p + x_sample.shape[0]
    xs = (x_prompt.reshape(nb_p * SEQ, D_MODEL), x_sample.reshape((nb - nb_p) * SEQ, D_MODEL))
    n_tiles = nb * SEQ // ROW_TILE
    tiles_p = nb_p * SEQ // ROW_TILE
    c = jnp.concatenate([c_prompt, c_sample], axis=0)

    ada = _ada_table(c, w_ada, b_ada).reshape(DEPTH, 6, nb, 1, D_MODEL)

    cos_r, sin_r = _rope_cos_sin(RET_QK_DIM)
    cos_r = jnp.concatenate([cos_r, cos_r], axis=-1)
    sin_r = jnp.concatenate([-sin_r, sin_r], axis=-1)
    cos_m, sin_m = _rope_cos_sin(MLA_ROPE)
    cs_m = jnp.concatenate([cos_m, cos_m, -sin_m, sin_m], axis=-1)

    w_in_e = _prep_w_in(w_in)
    w_uq_e = _prep_w_uq(w_uq)
    w_ukv = jnp.concatenate([w_uk, w_uv], axis=-1).astype(BF16)
    w_ro, w_mo, w_o = w_ret_o.astype(BF16), w_mla_o.astype(BF16), w_out.astype(BF16)
    w_u, w_d = w_up.astype(BF16), w_down.astype(BF16)
    dec_f = jnp.broadcast_to(ret_decay_fwd[:, :, None, None], (DEPTH, RET_HEADS, 1, RET_V_DIM))
    dec_b = jnp.broadcast_to(ret_decay_bwd[:, :, None, None], (DEPTH, RET_HEADS, 1, RET_V_DIM))

    for l in range(DEPTH):
        sh1, sc1, g1, sh2, sc2, g2 = (ada[l, k] for k in range(6))
        qkv, gates, qf, kf, vt = _in_proj(xs, sc1, sh1, w_in_e[l], cs_m, cos_r, sin_r, q_norm_g[l][None],
                                          kv_norm_g[l][None], w_uq_e[l], w_ukv[l])
        ron = _retention(qkv, dec_f[l], dec_b[l])
        att = _attention(qf, kf, vt)
        x = _merge(xs, ron, gates, att, g1, ret_gn_g[l][None], w_ro[l], w_mo[l], w_o[l],
                   ln1_g[l][None], ln1_b[l][None])
        ua, ub = _up_proj(x, sc2, sh2, w_u[l])
        ffn_out = functools.partial(_ffn_out, x, ua, ub, g2, conv_w[l], conv_b[l][None], w_d[l],
                                    ln2_g[l][None], ln2_b[l][None])
        if l < DEPTH - 1:
            xs = (ffn_out(0, n_tiles),)
        else:
            y_prompt = ffn_out(0, tiles_p)
            y_sample = ffn_out(tiles_p, n_tiles - tiles_p)

    return (y_prompt.reshape(nb_p, SEQ, D_MODEL), y_sample.reshape(nb - nb_p, SEQ, D_MODEL))
```

```python
import functools
import math

import jax
import jax.numpy as jnp
from jax import lax
from jax.experimental import pallas as pl
from jax.experimental.pallas import tpu as pltpu

D_MODEL = 1024
SEQ = 2048
DEPTH = 4
RET_HEADS = 4
RET_QK_DIM = 128
RET_V_DIM = 256
RET_QK_W = RET_HEADS * RET_QK_DIM
RET_V_W = RET_HEADS * RET_V_DIM
RET_CHUNK = 128
MLA_HEADS = 8
MLA_NOPE = 128
MLA_ROPE = 64
MLA_V = 128
Q_LORA = 256
KV_LORA = 128
MLA_QK = MLA_NOPE + MLA_ROPE
MLA_V_W = MLA_HEADS * MLA_V
MLA_HEAD_PAD = 256
D_FF = 2816
ROPE_THETA = 10000.0
LN_EPS = 1e-5
RMS_EPS = 1e-6
DEEPNORM_ALPHA = (2.0 * DEPTH) ** 0.25

IN_EXT_W = 5632
OFF_RQ, OFF_RK, OFF_RV, OFF_RG, OFF_GA, OFF_GB = 0, 512, 1024, 2048, 3072, 4096
OFF_DQ, OFF_DKV, OFF_KR = 5120, 5376, 5504
SRC_OFF_DQ, SRC_OFF_KR, SRC_OFF_GA, SRC_IN_W = 3072, 3456, 3520, 5568
PREP_ROWS = 256

V7X_VMEM_LIMIT_BYTES = 56 * 1024 * 1024
ROW_TILE = 512
UP_ROW_TILE = 1024
MM_COL_TILE = 512
FFN_K_TILE = 256
ATT_Q_TILE = 2048
ATT_SUB_TILE = 512
LOG2_E = 1.4426950408889634

BF16 = jnp.bfloat16
F32 = jnp.float32


def _cparams(*sem):
    return pltpu.CompilerParams(dimension_semantics=sem, vmem_limit_bytes=V7X_VMEM_LIMIT_BYTES)


def _resident(block_shape):
    return pl.BlockSpec(block_shape, lambda *_: (0,) * len(block_shape), pipeline_mode=pl.Buffered(1))


def _layernorm_rows(x):
    mu = jnp.mean(x, axis=-1, keepdims=True)
    xc = x - mu
    var = jnp.mean(xc * xc, axis=-1, keepdims=True)
    return xc * lax.rsqrt(var + LN_EPS)


def _ada_kernel(c_ref, w_ref, b_ref, o_ref):
    c = c_ref[...]
    s = (c * jax.nn.sigmoid(c)).astype(BF16)
    o_ref[...] = jnp.dot(s, w_ref[...].astype(BF16), preferred_element_type=F32) + b_ref[...]


def _ada_table(c, w_ada, b_ada):
    nb = c.shape[0]
    return pl.pallas_call(
        _ada_kernel,
        grid=(DEPTH, 6),
        in_specs=[
            pl.BlockSpec((nb, D_MODEL), lambda l, j: (0, 0)),
            pl.BlockSpec((None, D_MODEL, D_MODEL), lambda l, j: (l, 0, j)),
            pl.BlockSpec((None, 1, D_MODEL), lambda l, j: (l, 0, j)),
        ],
        out_specs=pl.BlockSpec((None, None, nb, D_MODEL), lambda l, j: (l, j, 0, 0)),
        out_shape=jax.ShapeDtypeStruct((DEPTH, 6, nb, D_MODEL), F32),
        compiler_params=_cparams("parallel", "parallel"),
        name="ada_table",
    )(c, w_ada, b_ada.reshape(DEPTH, 1, 6 * D_MODEL))


def _col_chunks(width):
    return [(c, min(MM_COL_TILE, width - c)) for c in range(0, width, MM_COL_TILE)]


def _up_proj_kernel(x_ref, sc_ref, sh_ref, w_ref, a_ref, b_ref, h_a, h_b):
    i = pl.program_id(0)

    @pl.when(i == 0)
    def _():
        h_b[...] = jnp.zeros(h_b.shape, BF16)

    def step(h_prev, h_cur):
        for c, wd in _col_chunks(D_FF):
            b_ref[:, c:c + wd] = jnp.dot(h_prev[...], w_ref[:, D_FF + c:D_FF + c + wd],
                                         preferred_element_type=F32).astype(b_ref.dtype)
        hb = (_layernorm_rows(x_ref[...]) * (1.0 + sc_ref[...]) + sh_ref[...]).astype(BF16)
        h_cur[...] = hb
        for c, wd in _col_chunks(D_FF):
            a_ref[:, c:c + wd] = jnp.dot(hb, w_ref[:, c:c + wd], preferred_element_type=F32).astype(a_ref.dtype)

    parity = lax.rem(i, 2)
    pl.when(parity == 0)(lambda: step(h_b, h_a))
    pl.when(parity == 1)(lambda: step(h_a, h_b))


def _up_proj(x, sc, sh, w):
    t = x.shape[0]
    n_tiles = t // UP_ROW_TILE
    tiles_per_seq = SEQ // UP_ROW_TILE
    cur = lambda i: jnp.minimum(i, n_tiles - 1)
    prv = lambda i: jnp.maximum(i - 1, 0)
    mod_spec = pl.BlockSpec((None, 1, D_MODEL), lambda i: (cur(i) // tiles_per_seq, 0, 0))
    h_buf = pltpu.VMEM((UP_ROW_TILE, D_MODEL), BF16)
    return pl.pallas_call(
        _up_proj_kernel,
        grid=(n_tiles + 1,),
        in_specs=[
            pl.BlockSpec((UP_ROW_TILE, D_MODEL), lambda i: (cur(i), 0)),
            mod_spec, mod_spec,
            _resident((D_MODEL, 2 * D_FF)),
        ],
        out_specs=[
            pl.BlockSpec((UP_ROW_TILE, D_FF), lambda i: (cur(i), 0)),
            pl.BlockSpec((UP_ROW_TILE, D_FF), lambda i: (prv(i), 0)),
        ],
        out_shape=[jax.ShapeDtypeStruct((t, D_FF), BF16), jax.ShapeDtypeStruct((t, D_FF), BF16)],
        scratch_shapes=[h_buf, h_buf],
        compiler_params=_cparams("arbitrary"),
        name="up_proj",
    )(x, sc, sh, w)


def _rms_rows(x, g):
    return x * lax.rsqrt(jnp.mean(x * x, axis=-1, keepdims=True) + RMS_EPS) * g


def _tile_rows(x_refs, i, split_tile):
    if split_tile is None:
        return x_refs[0][...]
    return jnp.where(i < split_tile, x_refs[0][...], x_refs[1][...])


def _tile_specs(xs, tile_of_step, split_tile):
    if split_tile is None:
        return [pl.BlockSpec((ROW_TILE, D_MODEL), lambda i: (tile_of_step(i), 0))]
    return [pl.BlockSpec((ROW_TILE, D_MODEL), lambda i: (jnp.minimum(tile_of_step(i), split_tile - 1), 0)),
            pl.BlockSpec((ROW_TILE, D_MODEL), lambda i: (jnp.maximum(tile_of_step(i) - split_tile, 0), 0))]


def _in_proj_kernel(*refs, split_tile):
    n_x = 1 if split_tile is None else 2
    x_refs = refs[:n_x]
    (sc_ref, sh_ref, w_ref, cs_ref, cos_ref, sin_ref, qg_ref, kvg_ref, wq_ref, wkv_ref,
     lo_ref, hi_ref, qf_ref, kf_ref, vt_ref, h_a, h_b) = refs[n_x:]
    i = pl.program_id(0)

    @pl.when(i == 0)
    def _():
        h_b[...] = jnp.zeros(h_b.shape, BF16)

    def step(h_prev, h_cur):
        for c, wd in _col_chunks(OFF_DQ - OFF_RG):
            hi_ref[:, c:c + wd] = jnp.dot(h_prev[...], w_ref[:, OFF_RG + c:OFF_RG + c + wd],
                                          preferred_element_type=F32).astype(hi_ref.dtype)
        x = _tile_rows(x_refs, i, split_tile)
        hb = (_layernorm_rows(x) * (1.0 + sc_ref[...]) + sh_ref[...]).astype(BF16)
        h_cur[...] = hb
        lat = jnp.dot(hb, w_ref[:, OFF_DQ:IN_EXT_W], preferred_element_type=F32)

        cos = cos_ref[...]
        sin = sin_ref[...]
        for c, wd in _col_chunks(OFF_RG):
            res = jnp.dot(hb, w_ref[:, c:c + wd], preferred_element_type=F32)
            if c < OFF_RV:
                gain = 1.0 if c < OFF_RK else RET_QK_DIM ** -0.5
                for h0 in range(0, wd, RET_QK_DIM):
                    xh = res[:, h0:h0 + RET_QK_DIM]
                    roped = (xh * cos + pltpu.roll(xh, RET_QK_DIM // 2, axis=1) * sin) * gain
                    lo_ref[:, c + h0:c + h0 + RET_QK_DIM] = roped.astype(lo_ref.dtype)
            else:
                lo_ref[:, c:c + wd] = res.astype(lo_ref.dtype)

        scale = MLA_QK ** -0.5 * LOG2_E
        cs = cs_ref[...]
        cq = _rms_rows(lat[:, 0:Q_LORA], qg_ref[...]).astype(BF16)
        ckv = _rms_rows(lat[:, Q_LORA:Q_LORA + KV_LORA], kvg_ref[...]).astype(BF16)
        tk = lat[:, Q_LORA + KV_LORA:] * cs
        lane = lax.broadcasted_iota(jnp.int32, tk.shape, 1)
        k_rope = jnp.where(lane < MLA_ROPE, tk + pltpu.roll(tk, MLA_ROPE, axis=1), 0.0).astype(BF16)
        kv = jnp.dot(ckv, wkv_ref[...], preferred_element_type=F32)
        vt_ref[...] = kv[:, MLA_V_W:].T.astype(BF16)
        for h in range(MLA_HEADS):
            base = h * MLA_HEAD_PAD
            qh = jnp.dot(cq, wq_ref[:, base:base + MLA_HEAD_PAD], preferred_element_type=F32)
            tq = qh[:, MLA_NOPE:] * cs
            q_rope = tq + pltpu.roll(tq, MLA_ROPE, axis=1)
            qf_ref[:, base:base + MLA_NOPE] = (qh[:, :MLA_NOPE] * scale).astype(BF16)
            qf_ref[:, base + MLA_NOPE:base + MLA_HEAD_PAD] = (q_rope * scale).astype(BF16)
            kf_ref[:, base:base + MLA_NOPE] = kv[:, h * MLA_NOPE:(h + 1) * MLA_NOPE].astype(BF16)
            kf_ref[:, base + MLA_NOPE:base + MLA_HEAD_PAD] = k_rope

    parity = lax.rem(i, 2)
    pl.when(parity == 0)(lambda: step(h_b, h_a))
    pl.when(parity == 1)(lambda: step(h_a, h_b))


def _in_proj(xs, sc, sh, w_in_e, cs_m, cos_r, sin_r, q_norm_g, kv_norm_g, wq_ext, wkv):
    t = sum(x.shape[0] for x in xs)
    split_tile = None if len(xs) == 1 else xs[0].shape[0] // ROW_TILE
    n_tiles = t // ROW_TILE
    tiles_per_seq = SEQ // ROW_TILE
    wide = MLA_HEADS * MLA_HEAD_PAD
    cur = lambda i: jnp.minimum(i, n_tiles - 1)
    prv = lambda i: jnp.maximum(i - 1, 0)
    row = lambda i: (cur(i), 0)
    pos = lambda i: (cur(i) % tiles_per_seq, 0)
    mod_spec = pl.BlockSpec((None, 1, D_MODEL), lambda i: (cur(i) // tiles_per_seq, 0, 0))
    h_buf = pltpu.VMEM((ROW_TILE, D_MODEL), BF16)
    return pl.pallas_call(
        functools.partial(_in_proj_kernel, split_tile=split_tile),
        grid=(n_tiles + 1,),
        in_specs=_tile_specs(xs, cur, split_tile) + [
            mod_spec, mod_spec,
            _resident((D_MODEL, IN_EXT_W)),
            pl.BlockSpec((ROW_TILE, 128), pos),
            pl.BlockSpec((ROW_TILE, RET_QK_DIM), pos),
            pl.BlockSpec((ROW_TILE, RET_QK_DIM), pos),
            _resident((1, Q_LORA)),
            _resident((1, KV_LORA)),
            _resident((Q_LORA, wide)),
            _resident((KV_LORA, 2 * MLA_V_W)),
        ],
        out_specs=[
            pl.BlockSpec((ROW_TILE, OFF_RG), row),
            pl.BlockSpec((ROW_TILE, OFF_DQ - OFF_RG), lambda i: (prv(i), 0)),
            pl.BlockSpec((ROW_TILE, wide), row),
            pl.BlockSpec((ROW_TILE, wide), row),
            pl.BlockSpec((None, MLA_V_W, ROW_TILE), lambda i: (cur(i) // tiles_per_seq, 0, cur(i) % tiles_per_seq)),
        ],
        out_shape=[
            jax.ShapeDtypeStruct((t, OFF_RG), BF16),
            jax.ShapeDtypeStruct((t, OFF_DQ - OFF_RG), BF16),
            jax.ShapeDtypeStruct((t, wide), BF16),
            jax.ShapeDtypeStruct((t, wide), BF16),
            jax.ShapeDtypeStruct((t // SEQ, MLA_V_W, SEQ), BF16),
        ],
        scratch_shapes=[h_buf, h_buf],
        compiler_params=_cparams("arbitrary"),
        name="in_proj",
    )(*xs, sc, sh, w_in_e, cs_m, cos_r, sin_r, q_norm_g, kv_norm_g, wq_ext, wkv)


def _retention_kernel(q_ref, k_ref, v_ref, df_ref, db_ref, o_ref, lhs_ref, rhs_ref, kvf_ref, kvb_ref):
    c = RET_CHUNK
    n_chunks = SEQ // c
    dk = RET_QK_DIM
    q = q_ref[...].astype(F32)
    kt = k_ref[...].astype(F32).T

    lg_f = jnp.log(jax.nn.sigmoid(df_ref[...]))
    lg_b = jnp.log(jax.nn.sigmoid(db_ref[...]))
    lgf = lg_f[:, :c]
    lgb = lg_b[:, :c]
    row = lax.broadcasted_iota(jnp.int32, (c, c), 0).astype(F32)
    col = lax.broadcasted_iota(jnp.int32, (c, c), 1).astype(F32)
    diff = row - col
    dmat = jnp.where(diff >= 0, jnp.exp(lgf * jnp.maximum(diff, 0.0)),
                     jnp.exp(lgb * jnp.maximum(-diff, 0.0)))
    qd_f = jnp.exp(lgf * (row + 1.0))
    qd_b = jnp.exp(lgb * (c - row))
    pos = col[0:1, :]
    kd_f = jnp.exp(lgf * (c - 1.0 - pos))
    kd_b = jnp.exp(lgb * pos)
    cd_f = jnp.exp(lg_f * float(c))
    cd_b = jnp.exp(lg_b * float(c))

    for n in range(n_chunks):
        rows = slice(n * c, (n + 1) * c)
        qn = q[rows, :]
        ktn = kt[:, rows]
        vn = v_ref[rows, :]
        s = jnp.dot(qn.astype(BF16), ktn.astype(BF16), preferred_element_type=F32) * dmat
        lhs_ref[rows, 0:c] = s.astype(BF16)
        lhs_ref[rows, c:c + dk] = (qn * qd_f).astype(BF16)
        lhs_ref[rows, c + dk:c + 2 * dk] = (qn * qd_b).astype(BF16)
        rhs_ref[n, 0:c, :] = vn
        kvf_ref[n] = jnp.dot((ktn * kd_f).astype(BF16), vn, preferred_element_type=F32)
        kvb_ref[n] = jnp.dot((ktn * kd_b).astype(BF16), vn, preferred_element_type=F32)

    state = jnp.zeros((dk, RET_V_DIM), F32)
    for n in range(n_chunks):
        rhs_ref[n, c:c + dk, :] = state.astype(BF16)
        state = cd_f * state + kvf_ref[n]
    state = jnp.zeros((dk, RET_V_DIM), F32)
    for n in reversed(range(n_chunks)):
        rhs_ref[n, c + dk:c + 2 * dk, :] = state.astype(BF16)
        state = cd_b * state + kvb_ref[n]

    for n in range(n_chunks):
        rows = slice(n * c, (n + 1) * c)
        o = jnp.dot(lhs_ref[rows, :], rhs_ref[n], preferred_element_type=F32)
        o_ref[rows, :] = _layernorm_rows(o).astype(o_ref.dtype)


def _retention(proj, dec_f, dec_b):
    t = proj.shape[0]
    nb = t // SEQ
    dec = pl.BlockSpec((None, 1, RET_V_DIM), lambda b, h: (h, 0, 0))
    return pl.pallas_call(
        _retention_kernel,
        grid=(nb, RET_HEADS),
        in_specs=[
            pl.BlockSpec((SEQ, RET_QK_DIM), lambda b, h: (b, OFF_RQ // RET_QK_DIM + h)),
            pl.BlockSpec((SEQ, RET_QK_DIM), lambda b, h: (b, OFF_RK // RET_QK_DIM + h)),
            pl.BlockSpec((SEQ, RET_V_DIM), lambda b, h: (b, OFF_RV // RET_V_DIM + h)),
            dec, dec,
        ],
        out_specs=pl.BlockSpec((SEQ, RET_V_DIM), lambda b, h: (b, h)),
        out_shape=jax.ShapeDtypeStruct((t, RET_V_W), BF16),
        scratch_shapes=[
            pltpu.VMEM((SEQ, RET_CHUNK + 2 * RET_QK_DIM), BF16),
            pltpu.VMEM((SEQ // RET_CHUNK, RET_CHUNK + 2 * RET_QK_DIM, RET_V_DIM), BF16),
            pltpu.VMEM((SEQ // RET_CHUNK, RET_QK_DIM, RET_V_DIM), F32),
            pltpu.VMEM((SEQ // RET_CHUNK, RET_QK_DIM, RET_V_DIM), F32),
        ],
        compiler_params=_cparams("parallel", "parallel"),
        name="retention",
    )(proj, proj, proj, dec_f, dec_b)


def _attention_kernel(q_ref, k_ref, vt_ref, o_ref, st_a, m_a, st_b, m_b):
    t = pl.program_id(0)

    @pl.when(t == 0)
    def _():
        st_b[...] = jnp.zeros(st_b.shape, F32)
        m_b[...] = jnp.zeros(m_b.shape, F32)

    def step(st_w, m_w, st_r, m_r):
        k = k_ref[...]
        vt = jnp.concatenate([vt_ref[...], jnp.ones((16, SEQ), BF16)], axis=0)
        for j in range(ATT_Q_TILE // ATT_SUB_TILE):
            rows = slice(j * ATT_SUB_TILE, (j + 1) * ATT_SUB_TILE)
            st = lax.dot_general(k, q_ref[rows, :], (((1,), (1,)), ((), ())),
                                 preferred_element_type=F32)
            st_w[j] = st
            m_w[j] = jnp.max(st, axis=0, keepdims=True)
            p = jnp.exp2(st_r[j] - m_r[j]).astype(BF16)
            ot = jnp.dot(vt, p, preferred_element_type=F32)
            o_ref[rows, :] = (ot[:MLA_V, :] / ot[MLA_V:MLA_V + 1, :]).T.astype(o_ref.dtype)

    parity = lax.rem(t, 2)
    pl.when(parity == 0)(lambda: step(st_a, m_a, st_b, m_b))
    pl.when(parity == 1)(lambda: step(st_b, m_b, st_a, m_a))


def _attention(qf, kf, vt):
    t = qf.shape[0]
    nb = t // SEQ
    q_tiles = SEQ // ATT_Q_TILE
    n_tiles = nb * MLA_HEADS * q_tiles
    n_sub = ATT_Q_TILE // ATT_SUB_TILE

    def tile_of(u):
        return u // (MLA_HEADS * q_tiles), (u // q_tiles) % MLA_HEADS, u % q_tiles

    def cur(s):
        return tile_of(jnp.minimum(s, n_tiles - 1))

    def prv(s):
        return tile_of(jnp.maximum(s - 1, 0))

    def q_map(s):
        b, h, i = cur(s)
        return (b * q_tiles + i, h)

    def k_map(s):
        b, h, _ = cur(s)
        return (b, h)

    def vt_map(s):
        b, h, _ = prv(s)
        return (b, h, 0)

    def o_map(s):
        b, h, i = prv(s)
        return (b * q_tiles + i, h)

    score_buf = pltpu.VMEM((n_sub, SEQ, ATT_SUB_TILE), F32)
    max_buf = pltpu.VMEM((n_sub, 1, ATT_SUB_TILE), F32)
    return pl.pallas_call(
        _attention_kernel,
        grid=(n_tiles + 1,),
        in_specs=[
            pl.BlockSpec((ATT_Q_TILE, MLA_HEAD_PAD), q_map),
            pl.BlockSpec((SEQ, MLA_HEAD_PAD), k_map),
            pl.BlockSpec((None, MLA_V, SEQ), vt_map),
        ],
        out_specs=pl.BlockSpec((ATT_Q_TILE, MLA_V), o_map),
        out_shape=jax.ShapeDtypeStruct((t, MLA_V_W), BF16),
        scratch_shapes=[score_buf, max_buf, score_buf, max_buf],
        compiler_params=_cparams("arbitrary"),
        name="mla_attention",
    )(qf, kf, vt)


def _merge_kernel(*refs, split_tile):
    n_x = 1 if split_tile is None else 2
    x_refs = refs[:n_x]
    (ron_ref, rg_ref, att_ref, ga_ref, gb_ref, g1_ref, gn_ref,
     wro_ref, wmo_ref, wout_ref, lng_ref, lnb_ref, o_ref) = refs[n_x:]
    rg = rg_ref[...].astype(F32)
    ya_in = (rg * jax.nn.sigmoid(rg)) * (ron_ref[...].astype(F32) * gn_ref[...])
    y_a = jnp.dot(ya_in.astype(BF16), wro_ref[...], preferred_element_type=F32)
    y_b = jnp.dot(att_ref[...], wmo_ref[...], preferred_element_type=F32)
    merged = (jax.nn.sigmoid(ga_ref[...].astype(F32)) * y_a
              + jax.nn.sigmoid(gb_ref[...].astype(F32)) * y_b)
    f = jnp.dot(merged.astype(BF16), wout_ref[...], preferred_element_type=F32)
    z = DEEPNORM_ALPHA * _tile_rows(x_refs, pl.program_id(0), split_tile) + (1.0 + g1_ref[...]) * f
    o_ref[...] = _layernorm_rows(z) * lng_ref[...] + lnb_ref[...]


def _merge(xs, ron, gates, att, g1, gn_g, w_ro, w_mo, w_out, ln_g, ln_b):
    t = sum(x.shape[0] for x in xs)
    split_tile = None if len(xs) == 1 else xs[0].shape[0] // ROW_TILE
    tiles_per_seq = SEQ // ROW_TILE
    row = lambda i: (i, 0)
    wide = pl.BlockSpec((ROW_TILE, D_MODEL), row)
    vec = _resident((1, D_MODEL))
    mat = _resident((D_MODEL, D_MODEL))
    return pl.pallas_call(
        functools.partial(_merge_kernel, split_tile=split_tile),
        grid=(t // ROW_TILE,),
        in_specs=_tile_specs(xs, lambda i: i, split_tile) + [
            wide,
            pl.BlockSpec((ROW_TILE, D_MODEL), lambda i: (i, 0)),
            wide,
            pl.BlockSpec((ROW_TILE, D_MODEL), lambda i: (i, (OFF_GA - OFF_RG) // D_MODEL)),
            pl.BlockSpec((ROW_TILE, D_MODEL), lambda i: (i, (OFF_GB - OFF_RG) // D_MODEL)),
            pl.BlockSpec((None, 1, D_MODEL), lambda i: (i // tiles_per_seq, 0, 0)),
            vec, mat, mat, mat, vec, vec,
        ],
        out_specs=wide,
        out_shape=jax.ShapeDtypeStruct((t, D_MODEL), F32),
        compiler_params=_cparams("parallel"),
        name="merge_out",
    )(*xs, ron, gates, att, gates, gates, g1, gn_g, w_ro, w_mo, w_out, ln_g, ln_b)


def _ffn_out_kernel(x_ref, a_ref, b_ref, ap_ref, an_ref, g2_ref, cw_ref, cb_ref, wd_ref,
                    lng_ref, lnb_ref, o_ref, *, first_tile):
    tiles_per_seq = SEQ // ROW_TILE
    i = pl.program_id(0) + first_tile
    tm = ROW_TILE
    first = (i % tiles_per_seq) == 0
    last = (i % tiles_per_seq) == tiles_per_seq - 1
    inv_sqrt2 = 1.0 / math.sqrt(2.0)
    cw = cw_ref[...] * inv_sqrt2
    cb = cb_ref[...] * inv_sqrt2
    prev_row = jnp.where(first, 0.0, ap_ref[7:8, :].astype(F32))
    next_row = jnp.where(last, 0.0, an_ref[0:1, :].astype(F32))
    r = lax.broadcasted_iota(jnp.int32, (tm, FFN_K_TILE), 0)
    y = None
    for c in range(D_FF // FFN_K_TILE):
        cols = slice(c * FFN_K_TILE, (c + 1) * FFN_K_TILE)
        a = a_ref[:, cols].astype(F32)
        a_up = jnp.where(r == 0, prev_row[:, cols], pltpu.roll(a, 1, axis=0))
        a_dn = jnp.where(r == tm - 1, next_row[:, cols], pltpu.roll(a, tm - 1, axis=0))
        conv = a_up * cw[0:1, cols] + a * cw[1:2, cols] + a_dn * cw[2:3, cols] + cb[:, cols]
        hmid = (conv * (1.0 + lax.erf(conv))).astype(BF16) * b_ref[:, cols]
        part = jnp.dot(hmid, wd_ref[cols, :], preferred_element_type=F32)
        y = part if y is None else y + part
    z = DEEPNORM_ALPHA * x_ref[...] + ((1.0 + g2_ref[...]) * inv_sqrt2) * y
    o_ref[...] = _layernorm_rows(z) * lng_ref[...] + lnb_ref[...]


def _ffn_out(x, ua, ub, g2, conv_w, conv_b, w_down, ln_g, ln_b, first_tile, n_tiles):
    tiles_per_seq = SEQ // ROW_TILE
    halo_per_tile = ROW_TILE // 8
    n_halo = x.shape[0] // 8
    row = lambda i: (i + first_tile, 0)
    return pl.pallas_call(
        functools.partial(_ffn_out_kernel, first_tile=first_tile),
        grid=(n_tiles,),
        in_specs=[
            pl.BlockSpec((ROW_TILE, D_MODEL), row),
            pl.BlockSpec((ROW_TILE, D_FF), lambda i: (i + first_tile, 0)),
            pl.BlockSpec((ROW_TILE, D_FF), lambda i: (i + first_tile, 0)),
            pl.BlockSpec((8, D_FF), lambda i: (jnp.maximum((i + first_tile) * halo_per_tile - 1, 0), 0)),
            pl.BlockSpec((8, D_FF),
                         lambda i: (jnp.minimum((i + first_tile + 1) * halo_per_tile, n_halo - 1), 0)),
            pl.BlockSpec((None, 1, D_MODEL), lambda i: ((i + first_tile) // tiles_per_seq, 0, 0)),
            _resident((3, D_FF)),
            _resident((1, D_FF)),
            _resident((D_FF, D_MODEL)),
            _resident((1, D_MODEL)),
            _resident((1, D_MODEL)),
        ],
        out_specs=pl.BlockSpec((ROW_TILE, D_MODEL), lambda i: (i, 0)),
        out_shape=jax.ShapeDtypeStruct((n_tiles * ROW_TILE, D_MODEL), F32),
        compiler_params=_cparams("parallel"),
        name="ffn_out",
    )(x, ua, ub, ua, ua, g2, conv_w, conv_b, w_down, ln_g, ln_b)


def _rope_cos_sin(dim):
    inv = 1.0 / (ROPE_THETA ** (jnp.arange(0, dim, 2, dtype=F32) / dim))
    ang = jnp.arange(SEQ, dtype=F32)[:, None] * inv[None, :]
    return jnp.cos(ang), jnp.sin(ang)


def _swap_halves(w):
    half = w.shape[-1] // 2
    return jnp.concatenate([w[..., half:], w[..., :half]], axis=-1)


def _w_in_prep_kernel(w_ref, o_ref):
    o_ref[:, 0:OFF_GA] = w_ref[:, 0:SRC_OFF_DQ].astype(BF16)
    o_ref[:, OFF_GA:OFF_DQ] = w_ref[:, SRC_OFF_GA:SRC_IN_W].astype(BF16)
    o_ref[:, OFF_DQ:OFF_KR] = w_ref[:, SRC_OFF_DQ:SRC_OFF_KR].astype(BF16)
    x = w_ref[:, SRC_OFF_KR:SRC_OFF_KR + 128]
    lane = lax.broadcasted_iota(jnp.int32, x.shape, 1)
    swapped = jnp.where(lane < MLA_ROPE + MLA_ROPE // 2,
                        pltpu.roll(x, MLA_ROPE // 2, axis=1), pltpu.roll(x, 128 - MLA_ROPE // 2, axis=1))
    o_ref[:, OFF_KR:IN_EXT_W] = jnp.where(lane < MLA_ROPE, x, swapped).astype(BF16)


def _prep_w_in(w_in):
    return pl.pallas_call(
        _w_in_prep_kernel,
        grid=(DEPTH, D_MODEL // PREP_ROWS),
        in_specs=[pl.BlockSpec((None, PREP_ROWS, SRC_IN_W), lambda l, r: (l, r, 0))],
        out_specs=pl.BlockSpec((None, PREP_ROWS, IN_EXT_W), lambda l, r: (l, r, 0)),
        out_shape=jax.ShapeDtypeStruct((DEPTH, D_MODEL, IN_EXT_W), BF16),
        compiler_params=_cparams("parallel", "parallel"),
        name="w_in_prep",
    )(w_in)


def _prep_w_uq(w_uq):
    w = w_uq.reshape(DEPTH, Q_LORA, MLA_HEADS, MLA_QK)
    rope = w[..., MLA_NOPE:]
    w = jnp.concatenate([w[..., :MLA_NOPE], rope, _swap_halves(rope)], axis=-1)
    return w.reshape(DEPTH, Q_LORA, MLA_HEADS * MLA_HEAD_PAD).astype(BF16)


def kernel(x_prompt, x_sample, c_prompt, c_sample, w_ada, b_ada, w_in, ret_decay_fwd, ret_decay_bwd,
           ret_gn_g, w_ret_o, q_norm_g, kv_norm_g, w_uq, w_uk, w_uv, w_mla_o, w_out, ln1_g, ln1_b,
           w_up, conv_w, conv_b, w_down, ln2_g, ln2_b):
    nb_p = x_prompt.shape[0]
    nb = nb_p + x_sample.shape[0]
    xs = (x_prompt.reshape(nb_p * SEQ, D_MODEL), x_sample.reshape((nb - nb_p) * SEQ, D_MODEL))
    n_tiles = nb * SEQ // ROW_TILE
    tiles_p = nb_p * SEQ // ROW_TILE
    c = jnp.concatenate([c_prompt, c_sample], axis=0)

    ada = _ada_table(c, w_ada, b_ada).reshape(DEPTH, 6, nb, 1, D_MODEL)

    cos_r, sin_r = _rope_cos_sin(RET_QK_DIM)
    cos_r = jnp.concatenate([cos_r, cos_r], axis=-1)
    sin_r = jnp.concatenate([-sin_r, sin_r], axis=-1)
    cos_m, sin_m = _rope_cos_sin(MLA_ROPE)
    cs_m = jnp.concatenate([cos_m, cos_m, -sin_m, sin_m], axis=-1)

    w_in_e = _prep_w_in(w_in)
    w_uq_e = _prep_w_uq(w_uq)
    w_ukv = jnp.concatenate([w_uk, w_uv], axis=-1).astype(BF16)
    w_ro, w_mo, w_o = w_ret_o.astype(BF16), w_mla_o.astype(BF16), w_out.astype(BF16)
    w_u, w_d = w_up.astype(BF16), w_down.astype(BF16)
    dec_f = jnp.broadcast_to(ret_decay_fwd[:, :, None, None], (DEPTH, RET_HEADS, 1, RET_V_DIM))
    dec_b = jnp.broadcast_to(ret_decay_bwd[:, :, None, None], (DEPTH, RET_HEADS, 1, RET_V_DIM))

    for l in range(DEPTH):
        sh1, sc1, g1, sh2, sc2, g2 = (ada[l, k] for k in range(6))
        qkv, gates, qf, kf, vt = _in_proj(xs, sc1, sh1, w_in_e[l], cs_m, cos_r, sin_r, q_norm_g[l][None],
                                          kv_norm_g[l][None], w_uq_e[l], w_ukv[l])
        ron = _retention(qkv, dec_f[l], dec_b[l])
        att = _attention(qf, kf, vt)
        x = _merge(xs, ron, gates, att, g1, ret_gn_g[l][None], w_ro[l], w_mo[l], w_o[l],
                   ln1_g[l][None], ln1_b[l][None])
        ua, ub = _up_proj(x, sc2, sh2, w_u[l])
        ffn_out = functools.partial(_ffn_out, x, ua, ub, g2, conv_w[l], conv_b[l][None], w_d[l],
                                    ln2_g[l][None], ln2_b[l][None])
        if l < DEPTH - 1:
            xs = (ffn_out(0, n_tiles),)
        else:
            y_prompt = ffn_out(0, tiles_p)
            y_sample = ffn_out(tiles_p, n_tiles - tiles_p)

    return (y_prompt.reshape(nb_p, SEQ, D_MODEL), y_sample.reshape(nb - nb_p, SEQ, D_MODEL))
```

```python
import functools
import math

import jax
import jax.numpy as jnp
from jax import lax
from jax.experimental import pallas as pl
from jax.experimental.pallas import tpu as pltpu

D_MODEL = 1024
SEQ = 2048
DEPTH = 4
RET_HEADS = 4
RET_QK_DIM = 128
RET_V_DIM = 256
RET_QK_W = RET_HEADS * RET_QK_DIM
RET_V_W = RET_HEADS * RET_V_DIM
RET_CHUNK = 128
MLA_HEADS = 8
MLA_NOPE = 128
MLA_ROPE = 64
MLA_V = 128
Q_LORA = 256
KV_LORA = 128
MLA_QK = MLA_NOPE + MLA_ROPE
MLA_V_W = MLA_HEADS * MLA_V
MLA_HEAD_PAD = 256
D_FF = 2816
ROPE_THETA = 10000.0
LN_EPS = 1e-5
RMS_EPS = 1e-6
DEEPNORM_ALPHA = (2.0 * DEPTH) ** 0.25

IN_EXT_W = 5632
OFF_RQ, OFF_RK, OFF_RV, OFF_RG, OFF_GA, OFF_GB = 0, 512, 1024, 2048, 3072, 4096
OFF_DQ, OFF_DKV, OFF_KR = 5120, 5376, 5504
SRC_OFF_DQ, SRC_OFF_KR, SRC_OFF_GA, SRC_IN_W = 3072, 3456, 3520, 5568
PREP_ROWS = 256

V7X_VMEM_LIMIT_BYTES = 56 * 1024 * 1024
ROW_TILE = 512
UP_ROW_TILE = 1024
MM_COL_TILE = 512
FFN_K_TILE = 256
ATT_Q_TILE = 2048
ATT_SUB_TILE = 512
LOG2_E = 1.4426950408889634

BF16 = jnp.bfloat16
F32 = jnp.float32


def _cparams(*sem):
    return pltpu.CompilerParams(dimension_semantics=sem, vmem_limit_bytes=V7X_VMEM_LIMIT_BYTES)


def _resident(block_shape, layer=None):
    if layer is None:
        return pl.BlockSpec(block_shape, lambda *_: (0,) * len(block_shape), pipeline_mode=pl.Buffered(1))
    return pl.BlockSpec((None,) + tuple(block_shape), lambda *_: (layer,) + (0,) * len(block_shape),
                        pipeline_mode=pl.Buffered(1))


def _layernorm_rows(x):
    mu = jnp.mean(x, axis=-1, keepdims=True)
    xc = x - mu
    var = jnp.mean(xc * xc, axis=-1, keepdims=True)
    return xc * lax.rsqrt(var + LN_EPS)


def _ada_kernel(c_ref, w_ref, b_ref, o_ref):
    c = c_ref[...]
    s = (c * jax.nn.sigmoid(c)).astype(BF16)
    o_ref[...] = jnp.dot(s, w_ref[...].astype(BF16), preferred_element_type=F32) + b_ref[...]


def _ada_table(c, w_ada, b_ada):
    nb = c.shape[0]
    return pl.pallas_call(
        _ada_kernel,
        grid=(DEPTH, 6),
        in_specs=[
            pl.BlockSpec((nb, D_MODEL), lambda l, j: (0, 0)),
            pl.BlockSpec((None, D_MODEL, D_MODEL), lambda l, j: (l, 0, j)),
            pl.BlockSpec((None, 1, D_MODEL), lambda l, j: (l, 0, j)),
        ],
        out_specs=pl.BlockSpec((None, None, nb, D_MODEL), lambda l, j: (l, j, 0, 0)),
        out_shape=jax.ShapeDtypeStruct((DEPTH, 6, nb, D_MODEL), F32),
        compiler_params=_cparams("parallel", "parallel"),
        name="ada_table",
    )(c, w_ada, b_ada.reshape(DEPTH, 1, 6 * D_MODEL))


def _col_chunks(width):
    return [(c, min(MM_COL_TILE, width - c)) for c in range(0, width, MM_COL_TILE)]


def _up_proj_kernel(x_ref, sc_ref, sh_ref, w_ref, a_ref, b_ref, h_a, h_b):
    i = pl.program_id(0)

    @pl.when(i == 0)
    def _():
        h_b[...] = jnp.zeros(h_b.shape, BF16)

    def step(h_prev, h_cur):
        for c, wd in _col_chunks(D_FF):
            b_ref[:, c:c + wd] = jnp.dot(h_prev[...], w_ref[:, D_FF + c:D_FF + c + wd],
                                         preferred_element_type=F32).astype(b_ref.dtype)
        hb = (_layernorm_rows(x_ref[...]) * (1.0 + sc_ref[...]) + sh_ref[...]).astype(BF16)
        h_cur[...] = hb
        for c, wd in _col_chunks(D_FF):
            a_ref[:, c:c + wd] = jnp.dot(hb, w_ref[:, c:c + wd], preferred_element_type=F32).astype(a_ref.dtype)

    parity = lax.rem(i, 2)
    pl.when(parity == 0)(lambda: step(h_b, h_a))
    pl.when(parity == 1)(lambda: step(h_a, h_b))


def _up_proj(x, sc, sh, w, layer):
    t = x.shape[0]
    n_tiles = t // UP_ROW_TILE
    tiles_per_seq = SEQ // UP_ROW_TILE
    cur = lambda i: jnp.minimum(i, n_tiles - 1)
    prv = lambda i: jnp.maximum(i - 1, 0)
    mod_spec = pl.BlockSpec((None, 1, D_MODEL), lambda i: (cur(i) // tiles_per_seq, 0, 0))
    h_buf = pltpu.VMEM((UP_ROW_TILE, D_MODEL), BF16)
    return pl.pallas_call(
        _up_proj_kernel,
        grid=(n_tiles + 1,),
        in_specs=[
            pl.BlockSpec((UP_ROW_TILE, D_MODEL), lambda i: (cur(i), 0)),
            mod_spec, mod_spec,
            _resident((D_MODEL, 2 * D_FF), layer),
        ],
        out_specs=[
            pl.BlockSpec((UP_ROW_TILE, D_FF), lambda i: (cur(i), 0)),
            pl.BlockSpec((UP_ROW_TILE, D_FF), lambda i: (prv(i), 0)),
        ],
        out_shape=[jax.ShapeDtypeStruct((t, D_FF), BF16), jax.ShapeDtypeStruct((t, D_FF), BF16)],
        scratch_shapes=[h_buf, h_buf],
        compiler_params=_cparams("arbitrary"),
        name="up_proj",
    )(x, sc, sh, w)


def _rms_rows(x, g):
    return x * lax.rsqrt(jnp.mean(x * x, axis=-1, keepdims=True) + RMS_EPS) * g


def _tile_rows(x_refs, i, split_tile):
    if split_tile is None:
        return x_refs[0][...]
    return jnp.where(i < split_tile, x_refs[0][...], x_refs[1][...])


def _tile_specs(xs, tile_of_step, split_tile):
    if split_tile is None:
        return [pl.BlockSpec((ROW_TILE, D_MODEL), lambda i: (tile_of_step(i), 0))]
    return [pl.BlockSpec((ROW_TILE, D_MODEL), lambda i: (jnp.minimum(tile_of_step(i), split_tile - 1), 0)),
            pl.BlockSpec((ROW_TILE, D_MODEL), lambda i: (jnp.maximum(tile_of_step(i) - split_tile, 0), 0))]


def _in_proj_kernel(*refs, split_tile):
    n_x = 1 if split_tile is None else 2
    x_refs = refs[:n_x]
    (sc_ref, sh_ref, w_ref, cs_ref, cos_ref, sin_ref, qg_ref, kvg_ref, wq_ref, wkv_ref,
     lo_ref, hi_ref, qf_ref, kf_ref, vt_ref, h_a, h_b) = refs[n_x:]
    i = pl.program_id(0)

    @pl.when(i == 0)
    def _():
        h_b[...] = jnp.zeros(h_b.shape, BF16)

    def step(h_prev, h_cur):
        for c, wd in _col_chunks(OFF_DQ - OFF_RG):
            hi_ref[:, c:c + wd] = jnp.dot(h_prev[...], w_ref[:, OFF_RG + c:OFF_RG + c + wd],
                                          preferred_element_type=F32).astype(hi_ref.dtype)
        x = _tile_rows(x_refs, i, split_tile)
        hb = (_layernorm_rows(x) * (1.0 + sc_ref[...]) + sh_ref[...]).astype(BF16)
        h_cur[...] = hb
        lat = jnp.dot(hb, w_ref[:, OFF_DQ:IN_EXT_W], preferred_element_type=F32)

        cos = cos_ref[...]
        sin = sin_ref[...]
        for c, wd in _col_chunks(OFF_RG):
            res = jnp.dot(hb, w_ref[:, c:c + wd], preferred_element_type=F32)
            if c < OFF_RV:
                gain = 1.0 if c < OFF_RK else RET_QK_DIM ** -0.5
                for h0 in range(0, wd, RET_QK_DIM):
                    xh = res[:, h0:h0 + RET_QK_DIM]
                    roped = (xh * cos + pltpu.roll(xh, RET_QK_DIM // 2, axis=1) * sin) * gain
                    lo_ref[:, c + h0:c + h0 + RET_QK_DIM] = roped.astype(lo_ref.dtype)
            else:
                lo_ref[:, c:c + wd] = res.astype(lo_ref.dtype)

        scale = MLA_QK ** -0.5 * LOG2_E
        cs = cs_ref[...]
        cq = _rms_rows(lat[:, 0:Q_LORA], qg_ref[...]).astype(BF16)
        ckv = _rms_rows(lat[:, Q_LORA:Q_LORA + KV_LORA], kvg_ref[...]).astype(BF16)
        tk = lat[:, Q_LORA + KV_LORA:] * cs
        lane = lax.broadcasted_iota(jnp.int32, tk.shape, 1)
        k_rope = jnp.where(lane < MLA_ROPE, tk + pltpu.roll(tk, MLA_ROPE, axis=1), 0.0).astype(BF16)
        kv = jnp.dot(ckv, wkv_ref[...], preferred_element_type=F32)
        vt_ref[...] = kv[:, MLA_V_W:].T.astype(BF16)
        for h in range(MLA_HEADS):
            base = h * MLA_HEAD_PAD
            qh = jnp.dot(cq, wq_ref[:, base:base + MLA_HEAD_PAD], preferred_element_type=F32)
            tq = qh[:, MLA_NOPE:] * cs
            q_rope = tq + pltpu.roll(tq, MLA_ROPE, axis=1)
            qf_ref[:, base:base + MLA_NOPE] = (qh[:, :MLA_NOPE] * scale).astype(BF16)
            qf_ref[:, base + MLA_NOPE:base + MLA_HEAD_PAD] = (q_rope * scale).astype(BF16)
            kf_ref[:, base:base + MLA_NOPE] = kv[:, h * MLA_NOPE:(h + 1) * MLA_NOPE].astype(BF16)
            kf_ref[:, base + MLA_NOPE:base + MLA_HEAD_PAD] = k_rope

    parity = lax.rem(i, 2)
    pl.when(parity == 0)(lambda: step(h_b, h_a))
    pl.when(parity == 1)(lambda: step(h_a, h_b))


def _in_proj(xs, sc, sh, w_in_e, cs_m, cos_r, sin_r, q_norm_g, kv_norm_g, wq_ext, wkv, layer):
    t = sum(x.shape[0] for x in xs)
    split_tile = None if len(xs) == 1 else xs[0].shape[0] // ROW_TILE
    n_tiles = t // ROW_TILE
    tiles_per_seq = SEQ // ROW_TILE
    wide = MLA_HEADS * MLA_HEAD_PAD
    cur = lambda i: jnp.minimum(i, n_tiles - 1)
    prv = lambda i: jnp.maximum(i - 1, 0)
    row = lambda i: (cur(i), 0)
    pos = lambda i: (cur(i) % tiles_per_seq, 0)
    mod_spec = pl.BlockSpec((None, 1, D_MODEL), lambda i: (cur(i) // tiles_per_seq, 0, 0))
    h_buf = pltpu.VMEM((ROW_TILE, D_MODEL), BF16)
    return pl.pallas_call(
        functools.partial(_in_proj_kernel, split_tile=split_tile),
        grid=(n_tiles + 1,),
        in_specs=_tile_specs(xs, cur, split_tile) + [
            mod_spec, mod_spec,
            _resident((D_MODEL, IN_EXT_W), layer),
            pl.BlockSpec((ROW_TILE, 128), pos),
            pl.BlockSpec((ROW_TILE, RET_QK_DIM), pos),
            pl.BlockSpec((ROW_TILE, RET_QK_DIM), pos),
            _resident((1, Q_LORA)),
            _resident((1, KV_LORA)),
            _resident((Q_LORA, wide), layer),
            _resident((KV_LORA, 2 * MLA_V_W), layer),
        ],
        out_specs=[
            pl.BlockSpec((ROW_TILE, OFF_RG), row),
            pl.BlockSpec((ROW_TILE, OFF_DQ - OFF_RG), lambda i: (prv(i), 0)),
            pl.BlockSpec((ROW_TILE, wide), row),
            pl.BlockSpec((ROW_TILE, wide), row),
            pl.BlockSpec((None, MLA_V_W, ROW_TILE), lambda i: (cur(i) // tiles_per_seq, 0, cur(i) % tiles_per_seq)),
        ],
        out_shape=[
            jax.ShapeDtypeStruct((t, OFF_RG), BF16),
            jax.ShapeDtypeStruct((t, OFF_DQ - OFF_RG), BF16),
            jax.ShapeDtypeStruct((t, wide), BF16),
            jax.ShapeDtypeStruct((t, wide), BF16),
            jax.ShapeDtypeStruct((t // SEQ, MLA_V_W, SEQ), BF16),
        ],
        scratch_shapes=[h_buf, h_buf],
        compiler_params=_cparams("arbitrary"),
        name="in_proj",
    )(*xs, sc, sh, w_in_e, cs_m, cos_r, sin_r, q_norm_g, kv_norm_g, wq_ext, wkv)


def _retention_kernel(q_ref, k_ref, v_ref, df_ref, db_ref, o_ref, lhs_ref, rhs_ref, kvf_ref, kvb_ref):
    c = RET_CHUNK
    n_chunks = SEQ // c
    dk = RET_QK_DIM
    q = q_ref[...].astype(F32)
    kt = k_ref[...].astype(F32).T

    lg_f = jnp.log(jax.nn.sigmoid(df_ref[...]))
    lg_b = jnp.log(jax.nn.sigmoid(db_ref[...]))
    lgf = lg_f[:, :c]
    lgb = lg_b[:, :c]
    row = lax.broadcasted_iota(jnp.int32, (c, c), 0).astype(F32)
    col = lax.broadcasted_iota(jnp.int32, (c, c), 1).astype(F32)
    diff = row - col
    dmat = jnp.where(diff >= 0, jnp.exp(lgf * jnp.maximum(diff, 0.0)),
                     jnp.exp(lgb * jnp.maximum(-diff, 0.0)))
    qd_f = jnp.exp(lgf * (row + 1.0))
    qd_b = jnp.exp(lgb * (c - row))
    pos = col[0:1, :]
    kd_f = jnp.exp(lgf * (c - 1.0 - pos))
    kd_b = jnp.exp(lgb * pos)
    cd_f = jnp.exp(lg_f * float(c))
    cd_b = jnp.exp(lg_b * float(c))

    for n in range(n_chunks):
        rows = slice(n * c, (n + 1) * c)
        qn = q[rows, :]
        ktn = kt[:, rows]
        vn = v_ref[rows, :]
        s = jnp.dot(qn.astype(BF16), ktn.astype(BF16), preferred_element_type=F32) * dmat
        lhs_ref[rows, 0:c] = s.astype(BF16)
        lhs_ref[rows, c:c + dk] = (qn * qd_f).astype(BF16)
        lhs_ref[rows, c + dk:c + 2 * dk] = (qn * qd_b).astype(BF16)
        rhs_ref[n, 0:c, :] = vn
        kvf_ref[n] = jnp.dot((ktn * kd_f).astype(BF16), vn, preferred_element_type=F32)
        kvb_ref[n] = jnp.dot((ktn * kd_b).astype(BF16), vn, preferred_element_type=F32)

    state = jnp.zeros((dk, RET_V_DIM), F32)
    for n in range(n_chunks):
        rhs_ref[n, c:c + dk, :] = state.astype(BF16)
        state = cd_f * state + kvf_ref[n]
    state = jnp.zeros((dk, RET_V_DIM), F32)
    for n in reversed(range(n_chunks)):
        rhs_ref[n, c + dk:c + 2 * dk, :] = state.astype(BF16)
        state = cd_b * state + kvb_ref[n]

    for n in range(n_chunks):
        rows = slice(n * c, (n + 1) * c)
        o = jnp.dot(lhs_ref[rows, :], rhs_ref[n], preferred_element_type=F32)
        o_ref[rows, :] = _layernorm_rows(o).astype(o_ref.dtype)


def _retention(proj, dec_f, dec_b):
    t = proj.shape[0]
    nb = t // SEQ
    dec = pl.BlockSpec((None, 1, RET_V_DIM), lambda b, h: (h, 0, 0))
    return pl.pallas_call(
        _retention_kernel,
        grid=(nb, RET_HEADS),
        in_specs=[
            pl.BlockSpec((SEQ, RET_QK_DIM), lambda b, h: (b, OFF_RQ // RET_QK_DIM + h)),
            pl.BlockSpec((SEQ, RET_QK_DIM), lambda b, h: (b, OFF_RK // RET_QK_DIM + h)),
            pl.BlockSpec((SEQ, RET_V_DIM), lambda b, h: (b, OFF_RV // RET_V_DIM + h)),
            dec, dec,
        ],
        out_specs=pl.BlockSpec((SEQ, RET_V_DIM), lambda b, h: (b, h)),
        out_shape=jax.ShapeDtypeStruct((t, RET_V_W), BF16),
        scratch_shapes=[
            pltpu.VMEM((SEQ, RET_CHUNK + 2 * RET_QK_DIM), BF16),
            pltpu.VMEM((SEQ // RET_CHUNK, RET_CHUNK + 2 * RET_QK_DIM, RET_V_DIM), BF16),
            pltpu.VMEM((SEQ // RET_CHUNK, RET_QK_DIM, RET_V_DIM), F32),
            pltpu.VMEM((SEQ // RET_CHUNK, RET_QK_DIM, RET_V_DIM), F32),
        ],
        compiler_params=_cparams("parallel", "parallel"),
        name="retention",
    )(proj, proj, proj, dec_f, dec_b)


def _attention_kernel(q_ref, k_ref, vt_ref, o_ref, st_a, m_a, st_b, m_b):
    t = pl.program_id(0)

    @pl.when(t == 0)
    def _():
        st_b[...] = jnp.zeros(st_b.shape, F32)
        m_b[...] = jnp.zeros(m_b.shape, F32)

    def step(st_w, m_w, st_r, m_r):
        k = k_ref[...]
        vt = jnp.concatenate([vt_ref[...], jnp.ones((16, SEQ), BF16)], axis=0)
        for j in range(ATT_Q_TILE // ATT_SUB_TILE):
            rows = slice(j * ATT_SUB_TILE, (j + 1) * ATT_SUB_TILE)
            st = lax.dot_general(k, q_ref[rows, :], (((1,), (1,)), ((), ())),
                                 preferred_element_type=F32)
            st_w[j] = st
            m_w[j] = jnp.max(st, axis=0, keepdims=True)
            p = jnp.exp2(st_r[j] - m_r[j]).astype(BF16)
            ot = jnp.dot(vt, p, preferred_element_type=F32)
            o_ref[rows, :] = (ot[:MLA_V, :] / ot[MLA_V:MLA_V + 1, :]).T.astype(o_ref.dtype)

    parity = lax.rem(t, 2)
    pl.when(parity == 0)(lambda: step(st_a, m_a, st_b, m_b))
    pl.when(parity == 1)(lambda: step(st_b, m_b, st_a, m_a))


def _attention(qf, kf, vt):
    t = qf.shape[0]
    nb = t // SEQ
    q_tiles = SEQ // ATT_Q_TILE
    n_tiles = nb * MLA_HEADS * q_tiles
    n_sub = ATT_Q_TILE // ATT_SUB_TILE

    def tile_of(u):
        return u // (MLA_HEADS * q_tiles), (u // q_tiles) % MLA_HEADS, u % q_tiles

    def cur(s):
        return tile_of(jnp.minimum(s, n_tiles - 1))

    def prv(s):
        return tile_of(jnp.maximum(s - 1, 0))

    def q_map(s):
        b, h, i = cur(s)
        return (b * q_tiles + i, h)

    def k_map(s):
        b, h, _ = cur(s)
        return (b, h)

    def vt_map(s):
        b, h, _ = prv(s)
        return (b, h, 0)

    def o_map(s):
        b, h, i = prv(s)
        return (b * q_tiles + i, h)

    score_buf = pltpu.VMEM((n_sub, SEQ, ATT_SUB_TILE), F32)
    max_buf = pltpu.VMEM((n_sub, 1, ATT_SUB_TILE), F32)
    return pl.pallas_call(
        _attention_kernel,
        grid=(n_tiles + 1,),
        in_specs=[
            pl.BlockSpec((ATT_Q_TILE, MLA_HEAD_PAD), q_map),
            pl.BlockSpec((SEQ, MLA_HEAD_PAD), k_map),
            pl.BlockSpec((None, MLA_V, SEQ), vt_map),
        ],
        out_specs=pl.BlockSpec((ATT_Q_TILE, MLA_V), o_map),
        out_shape=jax.ShapeDtypeStruct((t, MLA_V_W), BF16),
        scratch_shapes=[score_buf, max_buf, score_buf, max_buf],
        compiler_params=_cparams("arbitrary"),
        name="mla_attention",
    )(qf, kf, vt)


def _merge_kernel(*refs, split_tile):
    n_x = 1 if split_tile is None else 2
    x_refs = refs[:n_x]
    (ron_ref, rg_ref, att_ref, ga_ref, gb_ref, g1_ref, gn_ref,
     wro_ref, wmo_ref, wout_ref, lng_ref, lnb_ref, o_ref) = refs[n_x:]
    rg = rg_ref[...].astype(F32)
    ya_in = (rg * jax.nn.sigmoid(rg)) * (ron_ref[...].astype(F32) * gn_ref[...])
    y_a = jnp.dot(ya_in.astype(BF16), wro_ref[...], preferred_element_type=F32)
    y_b = jnp.dot(att_ref[...], wmo_ref[...], preferred_element_type=F32)
    merged = (jax.nn.sigmoid(ga_ref[...].astype(F32)) * y_a
              + jax.nn.sigmoid(gb_ref[...].astype(F32)) * y_b)
    f = jnp.dot(merged.astype(BF16), wout_ref[...], preferred_element_type=F32)
    z = DEEPNORM_ALPHA * _tile_rows(x_refs, pl.program_id(0), split_tile) + (1.0 + g1_ref[...]) * f
    o_ref[...] = _layernorm_rows(z) * lng_ref[...] + lnb_ref[...]


def _merge(xs, ron, gates, att, g1, gn_g, w_ro, w_mo, w_out, ln_g, ln_b, layer):
    t = sum(x.shape[0] for x in xs)
    split_tile = None if len(xs) == 1 else xs[0].shape[0] // ROW_TILE
    tiles_per_seq = SEQ // ROW_TILE
    row = lambda i: (i, 0)
    wide = pl.BlockSpec((ROW_TILE, D_MODEL), row)
    vec = _resident((1, D_MODEL))
    mat = _resident((D_MODEL, D_MODEL), layer)
    return pl.pallas_call(
        functools.partial(_merge_kernel, split_tile=split_tile),
        grid=(t // ROW_TILE,),
        in_specs=_tile_specs(xs, lambda i: i, split_tile) + [
            wide,
            pl.BlockSpec((ROW_TILE, D_MODEL), lambda i: (i, 0)),
            wide,
            pl.BlockSpec((ROW_TILE, D_MODEL), lambda i: (i, (OFF_GA - OFF_RG) // D_MODEL)),
            pl.BlockSpec((ROW_TILE, D_MODEL), lambda i: (i, (OFF_GB - OFF_RG) // D_MODEL)),
            pl.BlockSpec((None, 1, D_MODEL), lambda i: (i // tiles_per_seq, 0, 0)),
            vec, mat, mat, mat, vec, vec,
        ],
        out_specs=wide,
        out_shape=jax.ShapeDtypeStruct((t, D_MODEL), F32),
        compiler_params=_cparams("parallel"),
        name="merge_out",
    )(*xs, ron, gates, att, gates, gates, g1, gn_g, w_ro, w_mo, w_out, ln_g, ln_b)


def _ffn_out_kernel(x_ref, a_ref, b_ref, ap_ref, an_ref, g2_ref, cw_ref, cb_ref, wd_ref,
                    lng_ref, lnb_ref, o_ref, *, first_tile):
    tiles_per_seq = SEQ // ROW_TILE
    i = pl.program_id(0) + first_tile
    tm = ROW_TILE
    first = (i % tiles_per_seq) == 0
    last = (i % tiles_per_seq) == tiles_per_seq - 1
    inv_sqrt2 = 1.0 / math.sqrt(2.0)
    cw = cw_ref[...] * inv_sqrt2
    cb = cb_ref[...] * inv_sqrt2
    prev_row = jnp.where(first, 0.0, ap_ref[7:8, :].astype(F32))
    next_row = jnp.where(last, 0.0, an_ref[0:1, :].astype(F32))
    r = lax.broadcasted_iota(jnp.int32, (tm, FFN_K_TILE), 0)
    y = None
    for c in range(D_FF // FFN_K_TILE):
        cols = slice(c * FFN_K_TILE, (c + 1) * FFN_K_TILE)
        a = a_ref[:, cols].astype(F32)
        a_up = jnp.where(r == 0, prev_row[:, cols], pltpu.roll(a, 1, axis=0))
        a_dn = jnp.where(r == tm - 1, next_row[:, cols], pltpu.roll(a, tm - 1, axis=0))
        conv = a_up * cw[0:1, cols] + a * cw[1:2, cols] + a_dn * cw[2:3, cols] + cb[:, cols]
        hmid = (conv * (1.0 + lax.erf(conv))).astype(BF16) * b_ref[:, cols]
        part = jnp.dot(hmid, wd_ref[cols, :], preferred_element_type=F32)
        y = part if y is None else y + part
    z = DEEPNORM_ALPHA * x_ref[...] + ((1.0 + g2_ref[...]) * inv_sqrt2) * y
    o_ref[...] = _layernorm_rows(z) * lng_ref[...] + lnb_ref[...]


def _ffn_out(x, ua, ub, g2, conv_w, conv_b, w_down, ln_g, ln_b, layer, first_tile, n_tiles):
    tiles_per_seq = SEQ // ROW_TILE
    halo_per_tile = ROW_TILE // 8
    n_halo = x.shape[0] // 8
    row = lambda i: (i + first_tile, 0)
    return pl.pallas_call(
        functools.partial(_ffn_out_kernel, first_tile=first_tile),
        grid=(n_tiles,),
        in_specs=[
            pl.BlockSpec((ROW_TILE, D_MODEL), row),
            pl.BlockSpec((ROW_TILE, D_FF), lambda i: (i + first_tile, 0)),
            pl.BlockSpec((ROW_TILE, D_FF), lambda i: (i + first_tile, 0)),
            pl.BlockSpec((8, D_FF), lambda i: (jnp.maximum((i + first_tile) * halo_per_tile - 1, 0), 0)),
            pl.BlockSpec((8, D_FF),
                         lambda i: (jnp.minimum((i + first_tile + 1) * halo_per_tile, n_halo - 1), 0)),
            pl.BlockSpec((None, 1, D_MODEL), lambda i: ((i + first_tile) // tiles_per_seq, 0, 0)),
            _resident((3, D_FF)),
            _resident((1, D_FF)),
            _resident((D_FF, D_MODEL), layer),
            _resident((1, D_MODEL)),
            _resident((1, D_MODEL)),
        ],
        out_specs=pl.BlockSpec((ROW_TILE, D_MODEL), lambda i: (i, 0)),
        out_shape=jax.ShapeDtypeStruct((n_tiles * ROW_TILE, D_MODEL), F32),
        compiler_params=_cparams("parallel"),
        name="ffn_out",
    )(x, ua, ub, ua, ua, g2, conv_w, conv_b, w_down, ln_g, ln_b)


def _rope_cos_sin(dim):
    inv = 1.0 / (ROPE_THETA ** (jnp.arange(0, dim, 2, dtype=F32) / dim))
    ang = jnp.arange(SEQ, dtype=F32)[:, None] * inv[None, :]
    return jnp.cos(ang), jnp.sin(ang)


def _swap_halves(w):
    half = w.shape[-1] // 2
    return jnp.concatenate([w[..., half:], w[..., :half]], axis=-1)


def _w_in_prep_kernel(w_ref, o_ref):
    o_ref[:, 0:OFF_GA] = w_ref[:, 0:SRC_OFF_DQ].astype(BF16)
    o_ref[:, OFF_GA:OFF_DQ] = w_ref[:, SRC_OFF_GA:SRC_IN_W].astype(BF16)
    o_ref[:, OFF_DQ:OFF_KR] = w_ref[:, SRC_OFF_DQ:SRC_OFF_KR].astype(BF16)
    x = w_ref[:, SRC_OFF_KR:SRC_OFF_KR + 128]
    lane = lax.broadcasted_iota(jnp.int32, x.shape, 1)
    swapped = jnp.where(lane < MLA_ROPE + MLA_ROPE // 2,
                        pltpu.roll(x, MLA_ROPE // 2, axis=1), pltpu.roll(x, 128 - MLA_ROPE // 2, axis=1))
    o_ref[:, OFF_KR:IN_EXT_W] = jnp.where(lane < MLA_ROPE, x, swapped).astype(BF16)


def _prep_w_in(w_in):
    return pl.pallas_call(
        _w_in_prep_kernel,
        grid=(DEPTH, D_MODEL // PREP_ROWS),
        in_specs=[pl.BlockSpec((None, PREP_ROWS, SRC_IN_W), lambda l, r: (l, r, 0))],
        out_specs=pl.BlockSpec((None, PREP_ROWS, IN_EXT_W), lambda l, r: (l, r, 0)),
        out_shape=jax.ShapeDtypeStruct((DEPTH, D_MODEL, IN_EXT_W), BF16),
        compiler_params=_cparams("parallel", "parallel"),
        name="w_in_prep",
    )(w_in)


def _prep_w_uq(w_uq):
    w = w_uq.reshape(DEPTH, Q_LORA, MLA_HEADS, MLA_QK)
    rope = w[..., MLA_NOPE:]
    w = jnp.concatenate([w[..., :MLA_NOPE], rope, _swap_halves(rope)], axis=-1)
    return w.reshape(DEPTH, Q_LORA, MLA_HEADS * MLA_HEAD_PAD).astype(BF16)


def kernel(x_prompt, x_sample, c_prompt, c_sample, w_ada, b_ada, w_in, ret_decay_fwd, ret_decay_bwd,
           ret_gn_g, w_ret_o, q_norm_g, kv_norm_g, w_uq, w_uk, w_uv, w_mla_o, w_out, ln1_g, ln1_b,
           w_up, conv_w, conv_b, w_down, ln2_g, ln2_b):
    nb_p = x_prompt.shape[0]
    nb = nb_p + x_sample.shape[0]
    xs = (x_prompt.reshape(nb_p * SEQ, D_MODEL), x_sample.reshape((nb - nb_p) * SEQ, D_MODEL))
    n_tiles = nb * SEQ // ROW_TILE
    tiles_p = nb_p * SEQ // ROW_TILE
    c = jnp.concatenate([c_prompt, c_sample], axis=0)

    ada = _ada_table(c, w_ada, b_ada).reshape(DEPTH, 6, nb, 1, D_MODEL)

    cos_r, sin_r = _rope_cos_sin(RET_QK_DIM)
    cos_r = jnp.concatenate([cos_r, cos_r], axis=-1)
    sin_r = jnp.concatenate([-sin_r, sin_r], axis=-1)
    cos_m, sin_m = _rope_cos_sin(MLA_ROPE)
    cs_m = jnp.concatenate([cos_m, cos_m, -sin_m, sin_m], axis=-1)

    w_in_e = _prep_w_in(w_in)
    w_uq_e = _prep_w_uq(w_uq)
    w_ukv = jnp.concatenate([w_uk, w_uv], axis=-1).astype(BF16)
    w_ro, w_mo, w_o = w_ret_o.astype(BF16), w_mla_o.astype(BF16), w_out.astype(BF16)
    w_u, w_d = w_up.astype(BF16), w_down.astype(BF16)
    dec_f = jnp.broadcast_to(ret_decay_fwd[:, :, None, None], (DEPTH, RET_HEADS, 1, RET_V_DIM))
    dec_b = jnp.broadcast_to(ret_decay_bwd[:, :, None, None], (DEPTH, RET_HEADS, 1, RET_V_DIM))

    for l in range(DEPTH):
        sh1, sc1, g1, sh2, sc2, g2 = (ada[l, k] for k in range(6))
        qkv, gates, qf, kf, vt = _in_proj(xs, sc1, sh1, w_in_e, cs_m, cos_r, sin_r, q_norm_g[l][None],
                                          kv_norm_g[l][None], w_uq_e, w_ukv, l)
        ron = _retention(qkv, dec_f[l], dec_b[l])
        att = _attention(qf, kf, vt)
        x = _merge(xs, ron, gates, att, g1, ret_gn_g[l][None], w_ro, w_mo, w_o,
                   ln1_g[l][None], ln1_b[l][None], l)
        ua, ub = _up_proj(x, sc2, sh2, w_u, l)
        ffn_out = functools.partial(_ffn_out, x, ua, ub, g2, conv_w[l], conv_b[l][None], w_d,
                                    ln2_g[l][None], ln2_b[l][None], l)
        if l < DEPTH - 1:
            xs = (ffn_out(0, n_tiles),)
        else:
            y_prompt = ffn_out(0, tiles_p)
            y_sample = ffn_out(tiles_p, n_tiles - tiles_p)

    return (y_prompt.reshape(nb_p, SEQ, D_MODEL), y_sample.reshape(nb - nb_p, SEQ, D_MODEL))
```

```python
import functools
import math

import jax
import jax.numpy as jnp
from jax import lax
from jax.experimental import pallas as pl
from jax.experimental.pallas import tpu as pltpu

D_MODEL = 1024
SEQ = 2048
DEPTH = 4
RET_HEADS = 4
RET_QK_DIM = 128
RET_V_DIM = 256
RET_QK_W = RET_HEADS * RET_QK_DIM
RET_V_W = RET_HEADS * RET_V_DIM
RET_CHUNK = 128
MLA_HEADS = 8
MLA_NOPE = 128
MLA_ROPE = 64
MLA_V = 128
Q_LORA = 256
KV_LORA = 128
MLA_QK = MLA_NOPE + MLA_ROPE
MLA_V_W = MLA_HEADS * MLA_V
MLA_HEAD_PAD = 256
D_FF = 2816
ROPE_THETA = 10000.0
LN_EPS = 1e-5
RMS_EPS = 1e-6
DEEPNORM_ALPHA = (2.0 * DEPTH) ** 0.25

IN_EXT_W = 5632
OFF_RQ, OFF_RK, OFF_RV, OFF_RG, OFF_GA, OFF_GB = 0, 512, 1024, 2048, 3072, 4096
OFF_DQ, OFF_DKV, OFF_KR = 5120, 5376, 5504
SRC_OFF_DQ, SRC_OFF_KR, SRC_OFF_GA, SRC_IN_W = 3072, 3456, 3520, 5568
PREP_ROWS = 256

V7X_VMEM_LIMIT_BYTES = 56 * 1024 * 1024
ROW_TILE = 512
UP_ROW_TILE = 1024
MM_COL_TILE = 512
FFN_K_TILE = 256
MERGE_K_TILE = 256
ATT_Q_TILE = 2048
ATT_SUB_TILE = 512
LOG2_E = 1.4426950408889634

BF16 = jnp.bfloat16
F32 = jnp.float32


def _cparams(*sem):
    return pltpu.CompilerParams(dimension_semantics=sem, vmem_limit_bytes=V7X_VMEM_LIMIT_BYTES)


def _resident(block_shape, layer=None):
    if layer is None:
        return pl.BlockSpec(block_shape, lambda *_: (0,) * len(block_shape), pipeline_mode=pl.Buffered(1))
    return pl.BlockSpec((None,) + tuple(block_shape), lambda *_: (layer,) + (0,) * len(block_shape),
                        pipeline_mode=pl.Buffered(1))


def _layernorm_rows(x):
    mu = jnp.mean(x, axis=-1, keepdims=True)
    xc = x - mu
    var = jnp.mean(xc * xc, axis=-1, keepdims=True)
    return xc * lax.rsqrt(var + LN_EPS)


def _ada_kernel(c_ref, w_ref, b_ref, o_ref):
    c = c_ref[...]
    s = (c * jax.nn.sigmoid(c)).astype(BF16)
    o_ref[...] = jnp.dot(s, w_ref[...].astype(BF16), preferred_element_type=F32) + b_ref[...]


def _ada_table(c, w_ada, b_ada):
    nb = c.shape[0]
    return pl.pallas_call(
        _ada_kernel,
        grid=(DEPTH, 6),
        in_specs=[
            pl.BlockSpec((nb, D_MODEL), lambda l, j: (0, 0)),
            pl.BlockSpec((None, D_MODEL, D_MODEL), lambda l, j: (l, 0, j)),
            pl.BlockSpec((None, 1, D_MODEL), lambda l, j: (l, 0, j)),
        ],
        out_specs=pl.BlockSpec((None, None, nb, D_MODEL), lambda l, j: (l, j, 0, 0)),
        out_shape=jax.ShapeDtypeStruct((DEPTH, 6, nb, D_MODEL), F32),
        compiler_params=_cparams("parallel", "parallel"),
        name="ada_table",
    )(c, w_ada, b_ada.reshape(DEPTH, 1, 6 * D_MODEL))


def _col_chunks(width):
    return [(c, min(MM_COL_TILE, width - c)) for c in range(0, width, MM_COL_TILE)]


def _up_proj_kernel(x_ref, sc_ref, sh_ref, w_ref, a_ref, b_ref, h_a, h_b):
    i = pl.program_id(0)

    @pl.when(i == 0)
    def _():
        h_b[...] = jnp.zeros(h_b.shape, BF16)

    def step(h_prev, h_cur):
        for c, wd in _col_chunks(D_FF):
            b_ref[:, c:c + wd] = jnp.dot(h_prev[...], w_ref[:, D_FF + c:D_FF + c + wd],
                                         preferred_element_type=F32).astype(b_ref.dtype)
        hb = (_layernorm_rows(x_ref[...]) * (1.0 + sc_ref[...]) + sh_ref[...]).astype(BF16)
        h_cur[...] = hb
        for c, wd in _col_chunks(D_FF):
            a_ref[:, c:c + wd] = jnp.dot(hb, w_ref[:, c:c + wd], preferred_element_type=F32).astype(a_ref.dtype)

    parity = lax.rem(i, 2)
    pl.when(parity == 0)(lambda: step(h_b, h_a))
    pl.when(parity == 1)(lambda: step(h_a, h_b))


def _up_proj(x, sc, sh, w, layer):
    t = x.shape[0]
    n_tiles = t // UP_ROW_TILE
    tiles_per_seq = SEQ // UP_ROW_TILE
    cur = lambda i: jnp.minimum(i, n_tiles - 1)
    prv = lambda i: jnp.maximum(i - 1, 0)
    mod_spec = pl.BlockSpec((None, 1, D_MODEL), lambda i: (cur(i) // tiles_per_seq, 0, 0))
    h_buf = pltpu.VMEM((UP_ROW_TILE, D_MODEL), BF16)
    return pl.pallas_call(
        _up_proj_kernel,
        grid=(n_tiles + 1,),
        in_specs=[
            pl.BlockSpec((UP_ROW_TILE, D_MODEL), lambda i: (cur(i), 0)),
            mod_spec, mod_spec,
            _resident((D_MODEL, 2 * D_FF), layer),
        ],
        out_specs=[
            pl.BlockSpec((UP_ROW_TILE, D_FF), lambda i: (cur(i), 0)),
            pl.BlockSpec((UP_ROW_TILE, D_FF), lambda i: (prv(i), 0)),
        ],
        out_shape=[jax.ShapeDtypeStruct((t, D_FF), BF16), jax.ShapeDtypeStruct((t, D_FF), BF16)],
        scratch_shapes=[h_buf, h_buf],
        compiler_params=_cparams("arbitrary"),
        name="up_proj",
    )(x, sc, sh, w)


def _rms_rows(x, g):
    return x * lax.rsqrt(jnp.mean(x * x, axis=-1, keepdims=True) + RMS_EPS) * g


def _tile_rows(x_refs, i, split_tile):
    if split_tile is None:
        return x_refs[0][...]
    return jnp.where(i < split_tile, x_refs[0][...], x_refs[1][...])


def _tile_specs(xs, tile_of_step, split_tile):
    if split_tile is None:
        return [pl.BlockSpec((ROW_TILE, D_MODEL), lambda i: (tile_of_step(i), 0))]
    return [pl.BlockSpec((ROW_TILE, D_MODEL), lambda i: (jnp.minimum(tile_of_step(i), split_tile - 1), 0)),
            pl.BlockSpec((ROW_TILE, D_MODEL), lambda i: (jnp.maximum(tile_of_step(i) - split_tile, 0), 0))]


def _in_proj_kernel(*refs, split_tile):
    n_x = 1 if split_tile is None else 2
    x_refs = refs[:n_x]
    (sc_ref, sh_ref, w_ref, cs_ref, cos_ref, sin_ref, qg_ref, kvg_ref, wq_ref, wkv_ref,
     lo_ref, hi_ref, qf_ref, kf_ref, vt_ref, h_a, h_b) = refs[n_x:]
    i = pl.program_id(0)

    @pl.when(i == 0)
    def _():
        h_b[...] = jnp.zeros(h_b.shape, BF16)

    def step(h_prev, h_cur):
        for c, wd in _col_chunks(OFF_DQ - OFF_RG):
            hi_ref[:, c:c + wd] = jnp.dot(h_prev[...], w_ref[:, OFF_RG + c:OFF_RG + c + wd],
                                          preferred_element_type=F32).astype(hi_ref.dtype)
        x = _tile_rows(x_refs, i, split_tile)
        hb = (_layernorm_rows(x) * (1.0 + sc_ref[...]) + sh_ref[...]).astype(BF16)
        h_cur[...] = hb
        lat = jnp.dot(hb, w_ref[:, OFF_DQ:IN_EXT_W], preferred_element_type=F32)

        cos = cos_ref[...]
        sin = sin_ref[...]
        for c, wd in _col_chunks(OFF_RG):
            res = jnp.dot(hb, w_ref[:, c:c + wd], preferred_element_type=F32)
            if c < OFF_RV:
                gain = 1.0 if c < OFF_RK else RET_QK_DIM ** -0.5
                for h0 in range(0, wd, RET_QK_DIM):
                    xh = res[:, h0:h0 + RET_QK_DIM]
                    roped = (xh * cos + pltpu.roll(xh, RET_QK_DIM // 2, axis=1) * sin) * gain
                    lo_ref[:, c + h0:c + h0 + RET_QK_DIM] = roped.astype(lo_ref.dtype)
            else:
                lo_ref[:, c:c + wd] = res.astype(lo_ref.dtype)

        scale = MLA_QK ** -0.5 * LOG2_E
        cs = cs_ref[...]
        cq = _rms_rows(lat[:, 0:Q_LORA], qg_ref[...]).astype(BF16)
        ckv = _rms_rows(lat[:, Q_LORA:Q_LORA + KV_LORA], kvg_ref[...]).astype(BF16)
        tk = lat[:, Q_LORA + KV_LORA:] * cs
        lane = lax.broadcasted_iota(jnp.int32, tk.shape, 1)
        k_rope = jnp.where(lane < MLA_ROPE, tk + pltpu.roll(tk, MLA_ROPE, axis=1), 0.0).astype(BF16)
        kv = jnp.dot(ckv, wkv_ref[...], preferred_element_type=F32)
        vt_ref[...] = kv[:, MLA_V_W:].T.astype(BF16)
        for h in range(MLA_HEADS):
            base = h * MLA_HEAD_PAD
            qh = jnp.dot(cq, wq_ref[:, base:base + MLA_HEAD_PAD], preferred_element_type=F32)
            tq = qh[:, MLA_NOPE:] * cs
            q_rope = tq + pltpu.roll(tq, MLA_ROPE, axis=1)
            qf_ref[:, base:base + MLA_NOPE] = (qh[:, :MLA_NOPE] * scale).astype(BF16)
            qf_ref[:, base + MLA_NOPE:base + MLA_HEAD_PAD] = (q_rope * scale).astype(BF16)
            kf_ref[:, base:base + MLA_NOPE] = kv[:, h * MLA_NOPE:(h + 1) * MLA_NOPE].astype(BF16)
            kf_ref[:, base + MLA_NOPE:base + MLA_HEAD_PAD] = k_rope

    parity = lax.rem(i, 2)
    pl.when(parity == 0)(lambda: step(h_b, h_a))
    pl.when(parity == 1)(lambda: step(h_a, h_b))


def _in_proj(xs, sc, sh, w_in_e, cs_m, cos_r, sin_r, q_norm_g, kv_norm_g, wq_ext, wkv, layer):
    t = sum(x.shape[0] for x in xs)
    split_tile = None if len(xs) == 1 else xs[0].shape[0] // ROW_TILE
    n_tiles = t // ROW_TILE
    tiles_per_seq = SEQ // ROW_TILE
    wide = MLA_HEADS * MLA_HEAD_PAD
    cur = lambda i: jnp.minimum(i, n_tiles - 1)
    prv = lambda i: jnp.maximum(i - 1, 0)
    row = lambda i: (cur(i), 0)
    pos = lambda i: (cur(i) % tiles_per_seq, 0)
    mod_spec = pl.BlockSpec((None, 1, D_MODEL), lambda i: (cur(i) // tiles_per_seq, 0, 0))
    h_buf = pltpu.VMEM((ROW_TILE, D_MODEL), BF16)
    return pl.pallas_call(
        functools.partial(_in_proj_kernel, split_tile=split_tile),
        grid=(n_tiles + 1,),
        in_specs=_tile_specs(xs, cur, split_tile) + [
            mod_spec, mod_spec,
            _resident((D_MODEL, IN_EXT_W), layer),
            pl.BlockSpec((ROW_TILE, 128), pos),
            pl.BlockSpec((ROW_TILE, RET_QK_DIM), pos),
            pl.BlockSpec((ROW_TILE, RET_QK_DIM), pos),
            _resident((1, Q_LORA)),
            _resident((1, KV_LORA)),
            _resident((Q_LORA, wide), layer),
            _resident((KV_LORA, 2 * MLA_V_W), layer),
        ],
        out_specs=[
            pl.BlockSpec((ROW_TILE, OFF_RG), row),
            pl.BlockSpec((ROW_TILE, OFF_DQ - OFF_RG), lambda i: (prv(i), 0)),
            pl.BlockSpec((ROW_TILE, wide), row),
            pl.BlockSpec((ROW_TILE, wide), row),
            pl.BlockSpec((None, MLA_V_W, ROW_TILE), lambda i: (cur(i) // tiles_per_seq, 0, cur(i) % tiles_per_seq)),
        ],
        out_shape=[
            jax.ShapeDtypeStruct((t, OFF_RG), BF16),
            jax.ShapeDtypeStruct((t, OFF_DQ - OFF_RG), BF16),
            jax.ShapeDtypeStruct((t, wide), BF16),
            jax.ShapeDtypeStruct((t, wide), BF16),
            jax.ShapeDtypeStruct((t // SEQ, MLA_V_W, SEQ), BF16),
        ],
        scratch_shapes=[h_buf, h_buf],
        compiler_params=_cparams("arbitrary"),
        name="in_proj",
    )(*xs, sc, sh, w_in_e, cs_m, cos_r, sin_r, q_norm_g, kv_norm_g, wq_ext, wkv)


def _retention_kernel(q_ref, k_ref, v_ref, df_ref, db_ref, o_ref, lhs_ref, rhs_ref, kvf_ref, kvb_ref):
    c = RET_CHUNK
    n_chunks = SEQ // c
    dk = RET_QK_DIM
    q = q_ref[...].astype(F32)
    kt = k_ref[...].astype(F32).T

    lg_f = jnp.log(jax.nn.sigmoid(df_ref[...]))
    lg_b = jnp.log(jax.nn.sigmoid(db_ref[...]))
    lgf = lg_f[:, :c]
    lgb = lg_b[:, :c]
    row = lax.broadcasted_iota(jnp.int32, (c, c), 0).astype(F32)
    col = lax.broadcasted_iota(jnp.int32, (c, c), 1).astype(F32)
    diff = row - col
    dmat = jnp.where(diff >= 0, jnp.exp(lgf * jnp.maximum(diff, 0.0)),
                     jnp.exp(lgb * jnp.maximum(-diff, 0.0)))
    qd_f = jnp.exp(lgf * (row + 1.0))
    qd_b = jnp.exp(lgb * (c - row))
    pos = col[0:1, :]
    kd_f = jnp.exp(lgf * (c - 1.0 - pos))
    kd_b = jnp.exp(lgb * pos)
    cd_f = jnp.exp(lg_f * float(c))
    cd_b = jnp.exp(lg_b * float(c))

    for n in range(n_chunks):
        rows = slice(n * c, (n + 1) * c)
        qn = q[rows, :]
        ktn = kt[:, rows]
        vn = v_ref[rows, :]
        s = jnp.dot(qn.astype(BF16), ktn.astype(BF16), preferred_element_type=F32) * dmat
        lhs_ref[rows, 0:c] = s.astype(BF16)
        lhs_ref[rows, c:c + dk] = (qn * qd_f).astype(BF16)
        lhs_ref[rows, c + dk:c + 2 * dk] = (qn * qd_b).astype(BF16)
        rhs_ref[n, 0:c, :] = vn
        kvf_ref[n] = jnp.dot((ktn * kd_f).astype(BF16), vn, preferred_element_type=F32)
        kvb_ref[n] = jnp.dot((ktn * kd_b).astype(BF16), vn, preferred_element_type=F32)

    state = jnp.zeros((dk, RET_V_DIM), F32)
    for n in range(n_chunks):
        rhs_ref[n, c:c + dk, :] = state.astype(BF16)
        state = cd_f * state + kvf_ref[n]
    state = jnp.zeros((dk, RET_V_DIM), F32)
    for n in reversed(range(n_chunks)):
        rhs_ref[n, c + dk:c + 2 * dk, :] = state.astype(BF16)
        state = cd_b * state + kvb_ref[n]

    for n in range(n_chunks):
        rows = slice(n * c, (n + 1) * c)
        o = jnp.dot(lhs_ref[rows, :], rhs_ref[n], preferred_element_type=F32)
        o_ref[rows, :] = _layernorm_rows(o).astype(o_ref.dtype)


def _retention(proj, dec_f, dec_b):
    t = proj.shape[0]
    nb = t // SEQ
    dec = pl.BlockSpec((None, 1, RET_V_DIM), lambda b, h: (h, 0, 0))
    return pl.pallas_call(
        _retention_kernel,
        grid=(nb, RET_HEADS),
        in_specs=[
            pl.BlockSpec((SEQ, RET_QK_DIM), lambda b, h: (b, OFF_RQ // RET_QK_DIM + h)),
            pl.BlockSpec((SEQ, RET_QK_DIM), lambda b, h: (b, OFF_RK // RET_QK_DIM + h)),
            pl.BlockSpec((SEQ, RET_V_DIM), lambda b, h: (b, OFF_RV // RET_V_DIM + h)),
            dec, dec,
        ],
        out_specs=pl.BlockSpec((SEQ, RET_V_DIM), lambda b, h: (b, h)),
        out_shape=jax.ShapeDtypeStruct((t, RET_V_W), BF16),
        scratch_shapes=[
            pltpu.VMEM((SEQ, RET_CHUNK + 2 * RET_QK_DIM), BF16),
            pltpu.VMEM((SEQ // RET_CHUNK, RET_CHUNK + 2 * RET_QK_DIM, RET_V_DIM), BF16),
            pltpu.VMEM((SEQ // RET_CHUNK, RET_QK_DIM, RET_V_DIM), F32),
            pltpu.VMEM((SEQ // RET_CHUNK, RET_QK_DIM, RET_V_DIM), F32),
        ],
        compiler_params=_cparams("parallel", "parallel"),
        name="retention",
    )(proj, proj, proj, dec_f, dec_b)


def _attention_kernel(q_ref, k_ref, vt_ref, o_ref, st_a, m_a, st_b, m_b):
    t = pl.program_id(0)

    @pl.when(t == 0)
    def _():
        st_b[...] = jnp.zeros(st_b.shape, F32)
        m_b[...] = jnp.zeros(m_b.shape, F32)

    def step(st_w, m_w, st_r, m_r):
        k = k_ref[...]
        vt = jnp.concatenate([vt_ref[...], jnp.ones((16, SEQ), BF16)], axis=0)
        for j in range(ATT_Q_TILE // ATT_SUB_TILE):
            rows = slice(j * ATT_SUB_TILE, (j + 1) * ATT_SUB_TILE)
            st = lax.dot_general(k, q_ref[rows, :], (((1,), (1,)), ((), ())),
                                 preferred_element_type=F32)
            st_w[j] = st
            m_w[j] = jnp.max(st, axis=0, keepdims=True)
            p = jnp.exp2(st_r[j] - m_r[j]).astype(BF16)
            ot = jnp.dot(vt, p, preferred_element_type=F32)
            o_ref[rows, :] = (ot[:MLA_V, :] / ot[MLA_V:MLA_V + 1, :]).T.astype(o_ref.dtype)

    parity = lax.rem(t, 2)
    pl.when(parity == 0)(lambda: step(st_a, m_a, st_b, m_b))
    pl.when(parity == 1)(lambda: step(st_b, m_b, st_a, m_a))


def _attention(qf, kf, vt):
    t = qf.shape[0]
    nb = t // SEQ
    q_tiles = SEQ // ATT_Q_TILE
    n_tiles = nb * MLA_HEADS * q_tiles
    n_sub = ATT_Q_TILE // ATT_SUB_TILE

    def tile_of(u):
        return u // (MLA_HEADS * q_tiles), (u // q_tiles) % MLA_HEADS, u % q_tiles

    def cur(s):
        return tile_of(jnp.minimum(s, n_tiles - 1))

    def prv(s):
        return tile_of(jnp.maximum(s - 1, 0))

    def q_map(s):
        b, h, i = cur(s)
        return (b * q_tiles + i, h)

    def k_map(s):
        b, h, _ = cur(s)
        return (b, h)

    def vt_map(s):
        b, h, _ = prv(s)
        return (b, h, 0)

    def o_map(s):
        b, h, i = prv(s)
        return (b * q_tiles + i, h)

    score_buf = pltpu.VMEM((n_sub, SEQ, ATT_SUB_TILE), F32)
    max_buf = pltpu.VMEM((n_sub, 1, ATT_SUB_TILE), F32)
    return pl.pallas_call(
        _attention_kernel,
        grid=(n_tiles + 1,),
        in_specs=[
            pl.BlockSpec((ATT_Q_TILE, MLA_HEAD_PAD), q_map),
            pl.BlockSpec((SEQ, MLA_HEAD_PAD), k_map),
            pl.BlockSpec((None, MLA_V, SEQ), vt_map),
        ],
        out_specs=pl.BlockSpec((ATT_Q_TILE, MLA_V), o_map),
        out_shape=jax.ShapeDtypeStruct((t, MLA_V_W), BF16),
        scratch_shapes=[score_buf, max_buf, score_buf, max_buf],
        compiler_params=_cparams("arbitrary"),
        name="mla_attention",
    )(qf, kf, vt)


def _merge_kernel(*refs, split_tile):
    n_x = 1 if split_tile is None else 2
    x_refs = refs[:n_x]
    (ron_ref, rg_ref, att_ref, ga_ref, gb_ref, g1_ref, gn_ref,
     wro_ref, wmo_ref, wout_ref, lng_ref, lnb_ref, o_ref) = refs[n_x:]
    y_a = None
    for c in range(0, RET_V_W, MERGE_K_TILE):
        cols = slice(c, c + MERGE_K_TILE)
        rg = rg_ref[:, cols].astype(F32)
        ya_in = (rg * jax.nn.sigmoid(rg)) * (ron_ref[:, cols].astype(F32) * gn_ref[:, cols])
        part = jnp.dot(ya_in.astype(BF16), wro_ref[cols, :], preferred_element_type=F32)
        y_a = part if y_a is None else y_a + part
    y_b = jnp.dot(att_ref[...], wmo_ref[...], preferred_element_type=F32)
    f = None
    for c in range(0, D_MODEL, MERGE_K_TILE):
        cols = slice(c, c + MERGE_K_TILE)
        merged = (jax.nn.sigmoid(ga_ref[:, cols].astype(F32)) * y_a[:, cols]
                  + jax.nn.sigmoid(gb_ref[:, cols].astype(F32)) * y_b[:, cols])
        part = jnp.dot(merged.astype(BF16), wout_ref[cols, :], preferred_element_type=F32)
        f = part if f is None else f + part
    z = DEEPNORM_ALPHA * _tile_rows(x_refs, pl.program_id(0), split_tile) + (1.0 + g1_ref[...]) * f
    o_ref[...] = _layernorm_rows(z) * lng_ref[...] + lnb_ref[...]


def _merge(xs, ron, gates, att, g1, gn_g, w_ro, w_mo, w_out, ln_g, ln_b, layer):
    t = sum(x.shape[0] for x in xs)
    split_tile = None if len(xs) == 1 else xs[0].shape[0] // ROW_TILE
    tiles_per_seq = SEQ // ROW_TILE
    row = lambda i: (i, 0)
    wide = pl.BlockSpec((ROW_TILE, D_MODEL), row)
    vec = _resident((1, D_MODEL))
    mat = _resident((D_MODEL, D_MODEL), layer)
    return pl.pallas_call(
        functools.partial(_merge_kernel, split_tile=split_tile),
        grid=(t // ROW_TILE,),
        in_specs=_tile_specs(xs, lambda i: i, split_tile) + [
            wide,
            pl.BlockSpec((ROW_TILE, D_MODEL), lambda i: (i, 0)),
            wide,
            pl.BlockSpec((ROW_TILE, D_MODEL), lambda i: (i, (OFF_GA - OFF_RG) // D_MODEL)),
            pl.BlockSpec((ROW_TILE, D_MODEL), lambda i: (i, (OFF_GB - OFF_RG) // D_MODEL)),
            pl.BlockSpec((None, 1, D_MODEL), lambda i: (i // tiles_per_seq, 0, 0)),
            vec, mat, mat, mat, vec, vec,
        ],
        out_specs=wide,
        out_shape=jax.ShapeDtypeStruct((t, D_MODEL), F32),
        compiler_params=_cparams("parallel"),
        name="merge_out",
    )(*xs, ron, gates, att, gates, gates, g1, gn_g, w_ro, w_mo, w_out, ln_g, ln_b)


def _ffn_out_kernel(x_ref, a_ref, b_ref, ap_ref, an_ref, g2_ref, cw_ref, cb_ref, wd_ref,
                    lng_ref, lnb_ref, o_ref, *, first_tile):
    tiles_per_seq = SEQ // ROW_TILE
    i = pl.program_id(0) + first_tile
    tm = ROW_TILE
    first = (i % tiles_per_seq) == 0
    last = (i % tiles_per_seq) == tiles_per_seq - 1
    inv_sqrt2 = 1.0 / math.sqrt(2.0)
    cw = cw_ref[...] * inv_sqrt2
    cb = cb_ref[...] * inv_sqrt2
    prev_row = jnp.where(first, 0.0, ap_ref[7:8, :].astype(F32))
    next_row = jnp.where(last, 0.0, an_ref[0:1, :].astype(F32))
    r = lax.broadcasted_iota(jnp.int32, (tm, FFN_K_TILE), 0)
    y = None
    for c in range(D_FF // FFN_K_TILE):
        cols = slice(c * FFN_K_TILE, (c + 1) * FFN_K_TILE)
        a = a_ref[:, cols].astype(F32)
        a_up = jnp.where(r == 0, prev_row[:, cols], pltpu.roll(a, 1, axis=0))
        a_dn = jnp.where(r == tm - 1, next_row[:, cols], pltpu.roll(a, tm - 1, axis=0))
        conv = a_up * cw[0:1, cols] + a * cw[1:2, cols] + a_dn * cw[2:3, cols] + cb[:, cols]
        hmid = (conv * (1.0 + lax.erf(conv))).astype(BF16) * b_ref[:, cols]
        part = jnp.dot(hmid, wd_ref[cols, :], preferred_element_type=F32)
        y = part if y is None else y + part
    z = DEEPNORM_ALPHA * x_ref[...] + ((1.0 + g2_ref[...]) * inv_sqrt2) * y
    o_ref[...] = _layernorm_rows(z) * lng_ref[...] + lnb_ref[...]


def _ffn_out(x, ua, ub, g2, conv_w, conv_b, w_down, ln_g, ln_b, layer, first_tile, n_tiles):
    tiles_per_seq = SEQ // ROW_TILE
    halo_per_tile = ROW_TILE // 8
    n_halo = x.shape[0] // 8
    row = lambda i: (i + first_tile, 0)
    return pl.pallas_call(
        functools.partial(_ffn_out_kernel, first_tile=first_tile),
        grid=(n_tiles,),
        in_specs=[
            pl.BlockSpec((ROW_TILE, D_MODEL), row),
            pl.BlockSpec((ROW_TILE, D_FF), lambda i: (i + first_tile, 0)),
            pl.BlockSpec((ROW_TILE, D_FF), lambda i: (i + first_tile, 0)),
            pl.BlockSpec((8, D_FF), lambda i: (jnp.maximum((i + first_tile) * halo_per_tile - 1, 0), 0)),
            pl.BlockSpec((8, D_FF),
                         lambda i: (jnp.minimum((i + first_tile + 1) * halo_per_tile, n_halo - 1), 0)),
            pl.BlockSpec((None, 1, D_MODEL), lambda i: ((i + first_tile) // tiles_per_seq, 0, 0)),
            _resident((3, D_FF)),
            _resident((1, D_FF)),
            _resident((D_FF, D_MODEL), layer),
            _resident((1, D_MODEL)),
            _resident((1, D_MODEL)),
        ],
        out_specs=pl.BlockSpec((ROW_TILE, D_MODEL), lambda i: (i, 0)),
        out_shape=jax.ShapeDtypeStruct((n_tiles * ROW_TILE, D_MODEL), F32),
        compiler_params=_cparams("parallel"),
        name="ffn_out",
    )(x, ua, ub, ua, ua, g2, conv_w, conv_b, w_down, ln_g, ln_b)


def _rope_cos_sin(dim):
    inv = 1.0 / (ROPE_THETA ** (jnp.arange(0, dim, 2, dtype=F32) / dim))
    ang = jnp.arange(SEQ, dtype=F32)[:, None] * inv[None, :]
    return jnp.cos(ang), jnp.sin(ang)


def _swap_halves(w):
    half = w.shape[-1] // 2
    return jnp.concatenate([w[..., half:], w[..., :half]], axis=-1)


def _w_in_prep_kernel(w_ref, o_ref):
    o_ref[:, 0:OFF_GA] = w_ref[:, 0:SRC_OFF_DQ].astype(BF16)
    o_ref[:, OFF_GA:OFF_DQ] = w_ref[:, SRC_OFF_GA:SRC_IN_W].astype(BF16)
    o_ref[:, OFF_DQ:OFF_KR] = w_ref[:, SRC_OFF_DQ:SRC_OFF_KR].astype(BF16)
    x = w_ref[:, SRC_OFF_KR:SRC_OFF_KR + 128]
    lane = lax.broadcasted_iota(jnp.int32, x.shape, 1)
    swapped = jnp.where(lane < MLA_ROPE + MLA_ROPE // 2,
                        pltpu.roll(x, MLA_ROPE // 2, axis=1), pltpu.roll(x, 128 - MLA_ROPE // 2, axis=1))
    o_ref[:, OFF_KR:IN_EXT_W] = jnp.where(lane < MLA_ROPE, x, swapped).astype(BF16)


def _prep_w_in(w_in):
    return pl.pallas_call(
        _w_in_prep_kernel,
        grid=(DEPTH, D_MODEL // PREP_ROWS),
        in_specs=[pl.BlockSpec((None, PREP_ROWS, SRC_IN_W), lambda l, r: (l, r, 0))],
        out_specs=pl.BlockSpec((None, PREP_ROWS, IN_EXT_W), lambda l, r: (l, r, 0)),
        out_shape=jax.ShapeDtypeStruct((DEPTH, D_MODEL, IN_EXT_W), BF16),
        compiler_params=_cparams("parallel", "parallel"),
        name="w_in_prep",
    )(w_in)


def _prep_w_uq(w_uq):
    w = w_uq.reshape(DEPTH, Q_LORA, MLA_HEADS, MLA_QK)
    rope = w[..., MLA_NOPE:]
    w = jnp.concatenate([w[..., :MLA_NOPE], rope, _swap_halves(rope)], axis=-1)
    return w.reshape(DEPTH, Q_LORA, MLA_HEADS * MLA_HEAD_PAD).astype(BF16)


def kernel(x_prompt, x_sample, c_prompt, c_sample, w_ada, b_ada, w_in, ret_decay_fwd, ret_decay_bwd,
           ret_gn_g, w_ret_o, q_norm_g, kv_norm_g, w_uq, w_uk, w_uv, w_mla_o, w_out, ln1_g, ln1_b,
           w_up, conv_w, conv_b, w_down, ln2_g, ln2_b):
    nb_p = x_prompt.shape[0]
    nb = nb_p + x_sample.shape[0]
    xs = (x_prompt.reshape(nb_p * SEQ, D_MODEL), x_sample.reshape((nb - nb_p) * SEQ, D_MODEL))
    n_tiles = nb * SEQ // ROW_TILE
    tiles_p = nb_p * SEQ // ROW_TILE
    c = jnp.concatenate([c_prompt, c_sample], axis=0)

    ada = _ada_table(c, w_ada, b_ada).reshape(DEPTH, 6, nb, 1, D_MODEL)

    cos_r, sin_r = _rope_cos_sin(RET_QK_DIM)
    cos_r = jnp.concatenate([cos_r, cos_r], axis=-1)
    sin_r = jnp.concatenate([-sin_r, sin_r], axis=-1)
    cos_m, sin_m = _rope_cos_sin(MLA_ROPE)
    cs_m = jnp.concatenate([cos_m, cos_m, -sin_m, sin_m], axis=-1)

    w_in_e = _prep_w_in(w_in)
    w_uq_e = _prep_w_uq(w_uq)
    w_ukv = jnp.concatenate([w_uk, w_uv], axis=-1).astype(BF16)
    w_ro, w_mo, w_o = w_ret_o.astype(BF16), w_mla_o.astype(BF16), w_out.astype(BF16)
    w_u, w_d = w_up.astype(BF16), w_down.astype(BF16)
    dec_f = jnp.broadcast_to(ret_decay_fwd[:, :, None, None], (DEPTH, RET_HEADS, 1, RET_V_DIM))
    dec_b = jnp.broadcast_to(ret_decay_bwd[:, :, None, None], (DEPTH, RET_HEADS, 1, RET_V_DIM))

    for l in range(DEPTH):
        sh1, sc1, g1, sh2, sc2, g2 = (ada[l, k] for k in range(6))
        qkv, gates, qf, kf, vt = _in_proj(xs, sc1, sh1, w_in_e, cs_m, cos_r, sin_r, q_norm_g[l][None],
                                          kv_norm_g[l][None], w_uq_e, w_ukv, l)
        ron = _retention(qkv, dec_f[l], dec_b[l])
        att = _attention(qf, kf, vt)
        x = _merge(xs, ron, gates, att, g1, ret_gn_g[l][None], w_ro, w_mo, w_o,
                   ln1_g[l][None], ln1_b[l][None], l)
        ua, ub = _up_proj(x, sc2, sh2, w_u, l)
        ffn_out = functools.partial(_ffn_out, x, ua, ub, g2, conv_w[l], conv_b[l][None], w_d,
                                    ln2_g[l][None], ln2_b[l][None], l)
        if l < DEPTH - 1:
            xs = (ffn_out(0, n_tiles),)
        else:
            y_prompt = ffn_out(0, tiles_p)
            y_sample = ffn_out(tiles_p, n_tiles - tiles_p)

    return (y_prompt.reshape(nb_p, SEQ, D_MODEL), y_sample.reshape(nb - nb_p, SEQ, D_MODEL))
```
